```python
import math, functools
import jax, jax.numpy as jnp
from jax import lax
import numpy as np


D_MODEL = 2048
BATCH = 4
SEQ = 2048
DEPTH = 2
DEC_BATCH = 128
DEC_SEQ = 4
PAST_LEN = 16384
PAGE_SIZE = 128

BRANCH_DIM = 1024
N_BRANCH = 3
CONV_DIM = BRANCH_DIM
CONV_WIDTH = 3
GLA_HEADS = 4
GLA_DK = 128
GLA_DV = BRANCH_DIM // GLA_HEADS
GLA_LOWRANK = 16
GLA_TAU = 16.0
GLA_CHUNK = 64
MLA_HEADS = 8
MLA_Q_RANK = 512
MLA_KV_RANK = 256
MLA_NOPE = 128
MLA_ROPE = 64
MLA_V = BRANCH_DIM // MLA_HEADS
MLA_SCALE = (MLA_NOPE + MLA_ROPE) ** -0.5
ROPE_THETA = 10000.0
Q_BLOCK = 128
D_FF = -(-8 * D_MODEL // (3 * 256)) * 256
RMS_EPS = 1e-6
NEG_INF = -1e30
IN_SPLITS = (CONV_DIM, CONV_DIM, CONV_DIM,
             GLA_HEADS * GLA_DK, GLA_HEADS * GLA_DK, GLA_HEADS * GLA_DV, GLA_LOWRANK, GLA_HEADS * GLA_DV,
             MLA_Q_RANK, MLA_KV_RANK, MLA_ROPE,
             N_BRANCH * D_MODEL)
N_IN = sum(IN_SPLITS)

kernel_name = "hybrid_conv_gla_mla_decode_step"


def rms_norm(x, g):
    xf = x.astype(jnp.float32)
    y = xf * lax.rsqrt(jnp.mean(xf * xf, axis=-1, keepdims=True) + RMS_EPS)
    return (y * g.astype(jnp.float32)).astype(x.dtype)


def rope(x, pos):
    half = MLA_ROPE // 2
    inv_freq = ROPE_THETA ** (-jnp.arange(half, dtype=jnp.float32) / half)
    ang = pos.astype(jnp.float32)[:, None] * inv_freq[None, :]
    cos = jnp.cos(ang)[:, None, :]
    sin = jnp.sin(ang)[:, None, :]
    xf = x.astype(jnp.float32)
    x1, x2 = xf[..., :half], xf[..., half:]
    return jnp.concatenate([x1 * cos - x2 * sin, x2 * cos + x1 * sin], axis=-1).astype(x.dtype)


def short_conv(u, buf, w):
    s = u.shape[1]
    full = jnp.concatenate([buf.astype(u.dtype), u], axis=1)
    y = full[:, 0:s] * w[0]
    for j in range(1, CONV_WIDTH):
        y = y + full[:, j:j + s] * w[j]
    return y, full[:, -(CONV_WIDTH - 1):]


def gla_chunked(q, k, v, log_a, s0):
    b, s, h, dk = q.shape
    dv = v.shape[-1]
    c = math.gcd(s, GLA_CHUNK)
    n = s // c

    def to_chunks(t):
        return t.reshape(b, n, c, h, t.shape[-1]).transpose(1, 0, 3, 2, 4)

    mask = jnp.tril(jnp.ones((c, c), dtype=bool))

    def step(S, inp):
        qi, ki, vi, ai = inp
        L = jnp.cumsum(ai, axis=-2)
        L_last = L[..., -1:, :]
        q_t = qi * jnp.exp(L)
        k_t = ki * jnp.exp(-L)
        attn = jnp.where(mask, jnp.einsum("bhtd,bhsd->bhts", q_t, k_t), 0.0)
        o = jnp.einsum("bhtd,bhdv->bhtv", q_t, S) + jnp.einsum("bhts,bhsv->bhtv", attn, vi)
        k_dec = ki * jnp.exp(L_last - L)
        S_new = S * jnp.exp(L_last)[..., 0, :, None] + jnp.einsum("bhsd,bhsv->bhdv", k_dec, vi)
        return S_new, o

    s_fin, oc = lax.scan(step, s0, (to_chunks(q), to_chunks(k), to_chunks(v), to_chunks(log_a)))
    o = oc.transpose(1, 0, 3, 2, 4).reshape(b, s, h, dv)
    return o, s_fin


def gla_branch(q, k, v, a_low, g, wa2, ba, onorm, s0):
    b, s, _ = q.shape
    f32 = jnp.float32
    qh = q.reshape(b, s, GLA_HEADS, GLA_DK).astype(f32) * (GLA_DK ** -0.5)
    kh = k.reshape(b, s, GLA_HEADS, GLA_DK).astype(f32)
    vh = v.reshape(b, s, GLA_HEADS, GLA_DV).astype(f32)
    log_a = jax.nn.log_sigmoid((a_low @ wa2 + ba).astype(f32)).reshape(b, s, GLA_HEADS, GLA_DK) / GLA_TAU
    o, s_fin = gla_chunked(qh, kh, vh, log_a, s0.astype(f32))
    o = rms_norm(o, onorm).reshape(b, s, GLA_HEADS * GLA_DV).astype(g.dtype)
    return o * jax.nn.silu(g), s_fin.astype(s0.dtype)


def mla_project(cq, ckv, kr, pos, q_norm, kv_norm, wuq, wuk):
    b, s, _ = cq.shape
    q = (rms_norm(cq, q_norm) @ wuq).reshape(b, s, MLA_HEADS, MLA_NOPE + MLA_ROPE)
    q_nope = q[..., :MLA_NOPE]
    q_rope = rope(q[..., MLA_NOPE:], pos)
    q_lat = jnp.einsum("bshn,chn->bshc", q_nope, wuk)
    c = rms_norm(ckv, kv_norm)
    k_rope = rope(kr[:, :, None, :], pos)[:, :, 0, :]
    return q_lat, q_rope, c, k_rope


def mla_scores(q_lat, q_rope, c, kr):
    sc = jnp.einsum("bqhc,bkc->bhqk", q_lat, c) + jnp.einsum("bqhr,bkr->bhqk", q_rope, kr)
    return sc.astype(jnp.float32) * MLA_SCALE


def online_update(carry, sc, cv):
    m, l, acc = carry
    m_new = jnp.maximum(m, sc.max(-1))
    corr = jnp.exp(m - m_new)
    p = jnp.exp(sc - m_new[..., None])
    acc = acc * corr[..., None] + jnp.einsum("bhqk,bkc->bhqc", p, cv.astype(jnp.float32))
    return (m_new, l * corr + p.sum(-1), acc)


def mla_prompt_attend(q_lat, q_rope, c, kr):
    b, s, h, C = q_lat.shape
    nb = s // Q_BLOCK
    qb = q_lat.reshape(b, nb, Q_BLOCK, h, C).swapaxes(0, 1)
    rb = q_rope.reshape(b, nb, Q_BLOCK, h, MLA_ROPE).swapaxes(0, 1)
    kpos = jnp.arange(s)

    def block(args):
        i, ql, qr = args
        sc = mla_scores(ql, qr, c, kr)
        qpos = i * Q_BLOCK + jnp.arange(Q_BLOCK)
        sc = jnp.where(kpos[None, :] <= qpos[:, None], sc, NEG_INF)
        p = jax.nn.softmax(sc, axis=-1)
        return jnp.einsum("bhqk,bkc->bqhc", p.astype(c.dtype), c)

    o = lax.map(block, (jnp.arange(nb), qb, rb))
    return o.swapaxes(0, 1).reshape(b, s, h, C)


def mla_sample_attend(q_lat, q_rope, c_new, kr_new, cache, layer, page_table):
    b, sq, h, C = q_lat.shape
    init = (jnp.full((b, h, sq), NEG_INF, jnp.float32),
            jnp.zeros((b, h, sq), jnp.float32),
            jnp.zeros((b, h, sq, C), jnp.float32))

    def step(carry, pages):
        blk = cache[layer, pages]
        sc = mla_scores(q_lat, q_rope, blk[..., :C], blk[..., C:])
        return online_update(carry, sc, blk[..., :C]), None

    carry, _ = lax.scan(step, init, page_table.T)
    sc = mla_scores(q_lat, q_rope, c_new, kr_new)
    sc = jnp.where(jnp.tril(jnp.ones((sq, sq), dtype=bool)), sc, NEG_INF)
    m, l, acc = online_update(carry, sc, c_new)
    o = acc / l[..., None]
    return o.transpose(0, 2, 1, 3).astype(q_lat.dtype)


def decoder_layer(x, pos, conv_buf, gla_s0, attend, p):
    b, s, _ = x.shape
    h = rms_norm(x, p["norms"][0])
    z = h @ p["w_in"]
    split_idx = np.cumsum(IN_SPLITS)[:-1].tolist()
    (cB, cC, ch, gq, gk, gv, ga, gg, mq, mkv, mkr, gates) = jnp.split(z, split_idx, axis=-1)
    yc, conv_new = short_conv(cC * ch, conv_buf, p["conv_w"])
    y_conv = cB * yc
    y_gla, gla_new = gla_branch(gq, gk, gv, ga, gg, p["gla_wa2"], p["gla_ba"], p["gla_onorm"], gla_s0)
    q_lat, q_rope, c, kr = mla_project(mq, mkv, mkr, pos, p["mla_q_norm"], p["mla_kv_norm"],
                                       p["mla_wuq"], p["mla_wuk"])
    o_lat = attend(q_lat, q_rope, c, kr)
    y_mla = jnp.einsum("bshc,chv->bshv", o_lat, p["mla_wuv"]).reshape(b, s, BRANCH_DIM)
    ys = jnp.stack([y_conv, y_gla, y_mla], axis=2)
    proj = jnp.einsum("bsnc,ncd->bsnd", ys, p["w_branch"])
    gate = jax.nn.sigmoid(gates.reshape(b, s, N_BRANCH, D_MODEL))
    mix = (gate * proj).sum(axis=2) @ p["w_out"]
    x = x + rms_norm(mix, p["norms"][1])
    gu = rms_norm(x, p["norms"][2]) @ p["ffn_w_gu"]
    f = (jax.nn.silu(gu[..., :D_FF]) * gu[..., D_FF:]) @ p["ffn_w_down"]
    x = x + rms_norm(f, p["norms"][3])
    kv_rows = jnp.concatenate([c, kr], axis=-1)
    return x, kv_rows, gla_new, conv_new


def setup_inputs(seed: int = 0) -> dict:
    key = jax.random.key(seed)
    ks = jax.random.split(key, 24)
    n_pages = PAST_LEN // PAGE_SIZE
    n_pool = (DEC_BATCH * n_pages * 5) // 4
    f32 = jnp.float32

    def nrm(k, shape, scale):
        return jax.random.normal(k, shape, f32) * scale

    page_table = jax.random.permutation(ks[3], n_pool)[:DEC_BATCH * n_pages]
    page_table = page_table.reshape(DEC_BATCH, n_pages).astype(jnp.int32)
    return {
        "x_prompt": nrm(ks[0], (BATCH, SEQ, D_MODEL), 1.0),
        "x_sample": nrm(ks[1], (DEC_BATCH, DEC_SEQ, D_MODEL), 1.0),
        "cache_mla": nrm(ks[2], (DEPTH, n_pool, PAGE_SIZE, MLA_KV_RANK + MLA_ROPE), 1.0),
        "page_table": page_table,
        "state_gla": nrm(ks[4], (DEPTH, DEC_BATCH, GLA_HEADS, GLA_DK, GLA_DV), 1.0),
        "state_conv": nrm(ks[5], (DEPTH, DEC_BATCH, CONV_WIDTH - 1, CONV_DIM), 1.0),
        "norms": 1.0 + nrm(ks[6], (DEPTH, 4, D_MODEL), 0.02),
        "w_in": nrm(ks[7], (DEPTH, D_MODEL, N_IN), D_MODEL ** -0.5),
        "conv_w": nrm(ks[8], (DEPTH, CONV_WIDTH, CONV_DIM), CONV_WIDTH ** -0.5),
        "gla_wa2": nrm(ks[9], (DEPTH, GLA_LOWRANK, GLA_HEADS * GLA_DK), GLA_LOWRANK ** -0.5),
        "gla_ba": nrm(ks[10], (DEPTH, GLA_HEADS * GLA_DK), 0.1),
        "gla_onorm": 1.0 + nrm(ks[11], (DEPTH, GLA_DV), 0.02),
        "mla_q_norm": 1.0 + nrm(ks[12], (DEPTH, MLA_Q_RANK), 0.02),
        "mla_kv_norm": 1.0 + nrm(ks[13], (DEPTH, MLA_KV_RANK), 0.02),
        "mla_wuq": nrm(ks[14], (DEPTH, MLA_Q_RANK, MLA_HEADS * (MLA_NOPE + MLA_ROPE)), MLA_Q_RANK ** -0.5),
        "mla_wuk": nrm(ks[15], (DEPTH, MLA_KV_RANK, MLA_HEADS, MLA_NOPE), MLA_KV_RANK ** -0.5),
        "mla_wuv": nrm(ks[16], (DEPTH, MLA_KV_RANK, MLA_HEADS, MLA_V), MLA_KV_RANK ** -0.5),
        "w_branch": nrm(ks[17], (DEPTH, N_BRANCH, BRANCH_DIM, D_MODEL), BRANCH_DIM ** -0.5),
        "w_out": nrm(ks[18], (DEPTH, D_MODEL, D_MODEL), D_MODEL ** -0.5),
        "ffn_w_gu": nrm(ks[19], (DEPTH, D_MODEL, 2 * D_FF), D_MODEL ** -0.5),
        "ffn_w_down": nrm(ks[20], (DEPTH, D_FF, D_MODEL), D_FF ** -0.5),
    }


def reference(x_prompt, x_sample, cache_mla, page_table, state_gla, state_conv, norms, w_in, conv_w,
              gla_wa2, gla_ba, gla_onorm, mla_q_norm, mla_kv_norm, mla_wuq, mla_wuk, mla_wuv,
              w_branch, w_out, ffn_w_gu, ffn_w_down):
    bp, sp, _ = x_prompt.shape
    bs, ss, _ = x_sample.shape
    past_len = page_table.shape[1] * PAGE_SIZE
    pos_p = jnp.arange(sp)
    pos_s = past_len + jnp.arange(ss)
    xp, xs = x_prompt, x_sample
    kv_p, kv_s, gla_p, gla_s, conv_p, conv_s = [], [], [], [], [], []
    for l in range(DEPTH):
        p = {"norms": norms[l], "w_in": w_in[l], "conv_w": conv_w[l], "gla_wa2": gla_wa2[l],
             "gla_ba": gla_ba[l], "gla_onorm": gla_onorm[l], "mla_q_norm": mla_q_norm[l],
             "mla_kv_norm": mla_kv_norm[l], "mla_wuq": mla_wuq[l], "mla_wuk": mla_wuk[l],
             "mla_wuv": mla_wuv[l], "w_branch": w_branch[l], "w_out": w_out[l],
             "ffn_w_gu": ffn_w_gu[l], "ffn_w_down": ffn_w_down[l]}
        conv0 = jnp.zeros((bp, CONV_WIDTH - 1, CONV_DIM), x_prompt.dtype)
        gla0 = jnp.zeros((bp, GLA_HEADS, GLA_DK, GLA_DV), state_gla.dtype)
        xp, kvr, gst, cst = decoder_layer(xp, pos_p, conv0, gla0, mla_prompt_attend, p)
        kv_p.append(kvr); gla_p.append(gst); conv_p.append(cst)
        attend_s = functools.partial(mla_sample_attend, cache=cache_mla, layer=l, page_table=page_table)
        xs, kvr, gst, cst = decoder_layer(xs, pos_s, state_conv[l], state_gla[l], attend_s, p)
        kv_s.append(kvr); gla_s.append(gst); conv_s.append(cst.astype(state_conv.dtype))
    return (xp, xs, jnp.stack(kv_p), jnp.stack(kv_s), jnp.stack(gla_p), jnp.stack(gla_s),
            jnp.stack(conv_p), jnp.stack(conv_s))
```

```python
import functools
import math

import numpy as np
import jax
import jax.numpy as jnp
from jax import lax
from jax.experimental import pallas as pl
from jax.experimental.pallas import tpu as pltpu

F32 = jnp.float32
BF16 = jnp.bfloat16

D_MODEL = 2048
BRANCH_DIM = 1024
N_BRANCH = 3
CONV_DIM = BRANCH_DIM
CONV_WIDTH = 3
GLA_HEADS = 4
GLA_DK = 128
GLA_DV = BRANCH_DIM // GLA_HEADS
GLA_LOWRANK = 16
GLA_TAU = 16.0
GLA_CHUNK = 64
MLA_HEADS = 8
MLA_Q_RANK = 512
MLA_KV_RANK = 256
MLA_NOPE = 128
MLA_ROPE = 64
MLA_V = BRANCH_DIM // MLA_HEADS
MLA_SCALE = (MLA_NOPE + MLA_ROPE) ** -0.5
MLA_QK = MLA_KV_RANK + MLA_ROPE
ROPE_THETA = 10000.0
PAGE_SIZE = 128
D_FF = -(-8 * D_MODEL // (3 * 256)) * 256
RMS_EPS = 1e-6
NEG_INF = -1e30

LANES = 128
SUBLANES = 8
VMEM_LIMIT_BYTES = 56 * 1024 * 1024

Z_GATES = 0
Z_CONV = Z_GATES + N_BRANCH * D_MODEL
Z_GQ = Z_CONV + 3 * CONV_DIM
Z_GK = Z_GQ + GLA_HEADS * GLA_DK
Z_GV = Z_GK + GLA_HEADS * GLA_DK
Z_GG = Z_GV + GLA_HEADS * GLA_DV
Z_MQ = Z_GG + GLA_HEADS * GLA_DV
Z_MKV = Z_MQ + MLA_Q_RANK
Z_MISC = Z_MKV + MLA_KV_RANK
Z_END = Z_MISC + LANES
NZ_TILE = 512
NZ = -(-Z_END // NZ_TILE) * NZ_TILE
MISC_KR = 0
MISC_GA = MLA_ROPE


def _cparams(*sem):
    return pltpu.CompilerParams(dimension_semantics=sem, vmem_limit_bytes=VMEM_LIMIT_BYTES)


def _rms_rows(x, g):
    ms = jnp.mean(x * x, axis=-1, keepdims=True)
    return x * lax.rsqrt(ms + RMS_EPS) * g


def _norm_rows_to(dst_ref, src_ref, g_ref, rows, chunk=32):
    def body(c, carry):
        r0 = pl.multiple_of(c * chunk, chunk)
        dst_ref[pl.ds(r0, chunk), :] = _rms_rows(src_ref[pl.ds(r0, chunk), :], g_ref[...]).astype(dst_ref.dtype)
        return carry
    lax.fori_loop(0, rows // chunk, body, 0)


def _norm_mm_kernel(x_ref, g_ref, w_ref, o_ref, xn_ref):
    @pl.when(pl.program_id(1) == 0)
    def _():
        _norm_rows_to(xn_ref, x_ref, g_ref, x_ref.shape[0])
    o_ref[...] = jnp.dot(xn_ref[...], w_ref[...], preferred_element_type=F32)


def norm_matmul(x, g, w, *, tm, tn):
    m, k = x.shape
    n = w.shape[1]
    return pl.pallas_call(
        _norm_mm_kernel,
        grid=(m // tm, n // tn),
        in_specs=[pl.BlockSpec((tm, k), lambda i, j: (i, 0)),
                  pl.BlockSpec((1, k), lambda i, j: (0, 0)),
                  pl.BlockSpec((k, tn), lambda i, j: (0, j))],
        out_specs=pl.BlockSpec((tm, tn), lambda i, j: (i, j)),
        out_shape=jax.ShapeDtypeStruct((m, n), F32),
        scratch_shapes=[pltpu.VMEM((tm, k), BF16)],
        compiler_params=_cparams("parallel", "arbitrary"),
        name="in_proj",
    )(x, g.reshape(1, k), w)


def _conv_prompt_kernel(cb_ref, cc_ref, ch_ref, w_ref, y_ref, st_ref, carry_ref):
    i = pl.program_id(2)

    @pl.when(i == 0)
    def _():
        carry_ref[...] = jnp.zeros_like(carry_ref)

    u = cc_ref[...] * ch_ref[...]
    ts = u.shape[0]
    row = lax.broadcasted_iota(jnp.int32, u.shape, 0)
    prev1 = carry_ref[1:2, :]
    prev2 = carry_ref[0:1, :]
    um1 = jnp.where(row == 0, prev1, pltpu.roll(u, 1, axis=0))
    um2 = jnp.where(row == 0, prev2, jnp.where(row == 1, prev1, pltpu.roll(u, 2, axis=0)))
    w = w_ref[...]
    y = um2 * w[0:1, :] + um1 * w[1:2, :] + u * w[2:3, :]
    y_ref[...] = (cb_ref[...] * y).astype(y_ref.dtype)
    last = u[ts - 2:ts, :]
    carry_ref[...] = last

    @pl.when(i == pl.num_programs(2) - 1)
    def _():
        st_ref[...] = last


def conv_prompt(z3, conv_w, *, ts, tc):
    b, s, _ = z3.shape
    c = CONV_DIM
    base = Z_CONV // tc
    nc = c // tc

    def zspec(part):
        return pl.BlockSpec((None, ts, tc), lambda bi, ci, i, part=part: (bi, i, base + part * nc + ci))

    return pl.pallas_call(
        _conv_prompt_kernel,
        grid=(b, nc, s // ts),
        in_specs=[zspec(0), zspec(1), zspec(2),
                  pl.BlockSpec((CONV_WIDTH, tc), lambda bi, ci, i: (0, ci))],
        out_specs=[pl.BlockSpec((None, ts, tc), lambda bi, ci, i: (bi, i, ci)),
                   pl.BlockSpec((None, CONV_WIDTH - 1, tc), lambda bi, ci, i: (bi, 0, ci))],
        out_shape=[jax.ShapeDtypeStruct((b, s, c), BF16),
                   jax.ShapeDtypeStruct((b, CONV_WIDTH - 1, c), F32)],
        scratch_shapes=[pltpu.VMEM((CONV_WIDTH - 1, tc), F32)],
        compiler_params=_cparams("parallel", "parallel", "arbitrary"),
        name="conv_prompt",
    )(z3, z3, z3, conv_w)


def _conv_sample_kernel(*refs, n_tok):
    cb = refs[0:n_tok]
    cc = refs[n_tok:2 * n_tok]
    ch = refs[2 * n_tok:3 * n_tok]
    buf = refs[3 * n_tok:3 * n_tok + 2]
    w_ref = refs[3 * n_tok + 2]
    y_ref, st_ref = refs[3 * n_tok + 3:]
    c = w_ref.shape[1]
    w = w_ref[...]
    full = [buf[0][...], buf[1][...]] + [cc[t][...] * ch[t][...] for t in range(n_tok)]
    for t in range(n_tok):
        y = full[t] * w[0:1, :] + full[t + 1] * w[1:2, :] + full[t + 2] * w[2:3, :]
        y_ref[:, t * c:(t + 1) * c] = cb[t][...] * y
    st_ref[:, 0:c] = full[n_tok]
    st_ref[:, c:2 * c] = full[n_tok + 1]


def conv_sample(z2, buf2, conv_w, *, n_tok, tb):
    b = z2.shape[0]
    c = CONV_DIM
    per_tok = NZ // c
    base = Z_CONV // c

    def zspec(part, t):
        return pl.BlockSpec((tb, c), lambda i, part=part, t=t: (i, t * per_tok + base + part))

    in_specs = [zspec(part, t) for part in range(3) for t in range(n_tok)]
    in_specs += [pl.BlockSpec((tb, c), lambda i, j=j: (i, j)) for j in range(2)]
    in_specs += [pl.BlockSpec((CONV_WIDTH, c), lambda i: (0, 0))]
    return pl.pallas_call(
        functools.partial(_conv_sample_kernel, n_tok=n_tok),
        grid=(b // tb,),
        in_specs=in_specs,
        out_specs=[pl.BlockSpec((tb, n_tok * c), lambda i: (i, 0)),
                   pl.BlockSpec((tb, 2 * c), lambda i: (i, 0))],
        out_shape=[jax.ShapeDtypeStruct((b, n_tok * c), F32),
                   jax.ShapeDtypeStruct((b, 2 * c), F32)],
        compiler_params=_cparams("parallel"),
        name="conv_sample",
    )(*([z2] * (3 * n_tok)), buf2, buf2, conv_w)


def _log_sigmoid(x):
    return jnp.minimum(x, 0.0) - jnp.log1p(jnp.exp(-jnp.abs(x)))


def _split3(x):
    hi = x.astype(BF16)
    r1 = x - hi.astype(F32)
    mid = r1.astype(BF16)
    lo = (r1 - mid.astype(F32)).astype(BF16)
    return hi, mid, lo


def _col_bcast(row, width):
    n = row.shape[1]
    sq = jnp.transpose(jnp.broadcast_to(row, (n, n)))
    return jnp.concatenate([sq] * (width // n), axis=1)


def _dot_nt(a, b):
    return lax.dot_general(a, b, (((1,), (1,)), ((), ())), preferred_element_type=F32)


def _dot_tn(a, b):
    return lax.dot_general(a, b, (((0,), (0,)), ((), ())), preferred_element_type=F32)


def _gla_gate_norm(o, g, onorm):
    on = o * lax.rsqrt(jnp.mean(o * o, axis=-1, keepdims=True) + RMS_EPS) * onorm
    return on * (g * jax.nn.sigmoid(g))


def _pad_rows(x, rows):
    if x.shape[0] == rows:
        return x
    return jnp.concatenate([x, jnp.zeros((rows - x.shape[0], x.shape[1]), x.dtype)], axis=0)


def _gla_log_decay(misc, wa2_ref, ba_ref):
    lane = lax.broadcasted_iota(jnp.int32, misc.shape, 1)
    a_low = jnp.where((lane >= MISC_GA) & (lane < MISC_GA + GLA_LOWRANK), misc, 0.0).astype(BF16)
    a = jnp.dot(a_low, wa2_ref[...], preferred_element_type=F32) + ba_ref[...]
    return _log_sigmoid(a) * (1.0 / GLA_TAU)


def _gla_prompt_kernel(q_ref, k_ref, v_ref, g_ref, a_ref, wa2_ref, ba_ref, on_ref, y_ref, st_ref, s_ref, *, ts):
    i = pl.program_id(1)

    @pl.when(i == 0)
    def _():
        s_ref[...] = jnp.zeros_like(s_ref)

    c = GLA_CHUNK
    pc = LANES
    rr = lax.broadcasted_iota(jnp.int32, (c, pc), 0)
    cc = lax.broadcasted_iota(jnp.int32, (c, pc), 1)
    tril = rr >= cc
    tril_b = tril.astype(BF16)
    scale = GLA_DK ** -0.5
    for ci in range(ts // c):
        r = slice(ci * c, (ci + 1) * c)
        la = _gla_log_decay(a_ref[r, :], wa2_ref, ba_ref)
        big_l = None
        for term in _split3(_pad_rows(la, pc)):
            part = jnp.dot(tril_b, term, preferred_element_type=F32)
            big_l = part if big_l is None else big_l + part
        l_last = big_l[c - 1:c, :]
        e_pos = jnp.exp(big_l)
        e_neg = jnp.exp(-big_l)
        e_dec = jnp.exp(l_last - big_l)
        e_last = jnp.exp(l_last)
        for h in range(GLA_HEADS):
            hs = slice(h * GLA_DK, (h + 1) * GLA_DK)
            vs = slice(h * GLA_DV, (h + 1) * GLA_DV)
            k = k_ref[r, hs]
            q_t = (q_ref[r, hs] * scale * e_pos[:, hs]).astype(BF16)
            k_t = _pad_rows(k * e_neg[:, hs], pc).astype(BF16)
            k_dec = _pad_rows(k * e_dec[:, hs], pc).astype(BF16)
            v = _pad_rows(v_ref[r, vs], pc).astype(BF16)
            attn = jnp.where(tril, _dot_nt(q_t, k_t), 0.0).astype(BF16)
            s_old = s_ref[h]
            o = (jnp.dot(q_t, s_old.astype(BF16), preferred_element_type=F32)
                 + jnp.dot(attn, v, preferred_element_type=F32))
            s_ref[h] = s_old * _col_bcast(e_last[:, hs], GLA_DV) + _dot_tn(k_dec, v)
            y_ref[r, vs] = _gla_gate_norm(o, g_ref[r, vs], on_ref[...]).astype(y_ref.dtype)

    @pl.when(i == pl.num_programs(1) - 1)
    def _():
        st_ref[...] = s_ref[...]


def gla_prompt(z3, wa2, ba, onorm, *, ts):
    b, s, _ = z3.shape
    hk = GLA_HEADS * GLA_DK
    hv = GLA_HEADS * GLA_DV
    return pl.pallas_call(
        functools.partial(_gla_prompt_kernel, ts=ts),
        grid=(b, s // ts),
        in_specs=[pl.BlockSpec((None, ts, hk), lambda bi, i: (bi, i, Z_GQ // hk)),
                  pl.BlockSpec((None, ts, hk), lambda bi, i: (bi, i, Z_GK // hk)),
                  pl.BlockSpec((None, ts, hv), lambda bi, i: (bi, i, Z_GV // hv)),
                  pl.BlockSpec((None, ts, hv), lambda bi, i: (bi, i, Z_GG // hv)),
                  pl.BlockSpec((None, ts, LANES), lambda bi, i: (bi, i, Z_MISC // LANES)),
                  pl.BlockSpec((LANES, hk), lambda bi, i: (0, 0)),
                  pl.BlockSpec((1, hk), lambda bi, i: (0, 0)),
                  pl.BlockSpec((1, GLA_DV), lambda bi, i: (0, 0))],
        out_specs=[pl.BlockSpec((None, ts, hv), lambda bi, i: (bi, i, 0)),
                   pl.BlockSpec((None, GLA_HEADS, GLA_DK, GLA_DV), lambda bi, i: (bi, 0, 0, 0))],
        out_shape=[jax.ShapeDtypeStruct((b, s, hv), BF16),
                   jax.ShapeDtypeStruct((b, GLA_HEADS, GLA_DK, GLA_DV), F32)],
        scratch_shapes=[pltpu.VMEM((GLA_HEADS, GLA_DK, GLA_DV), F32)],
        compiler_params=_cparams("parallel", "arbitrary"),
        name="gla_prompt",
    )(z3, z3, z3, z3, z3, wa2, ba.reshape(1, hk), onorm.reshape(1, GLA_DV))


def _gla_sample_kernel(q_ref, k_ref, v_ref, g_ref, a_ref, wa2_ref, ba_ref, on_ref, s0_ref, y_ref, st_ref,
                       *, n_tok, n_groups):
    rows = SUBLANES
    per = rows // n_tok
    pr = 2 * SUBLANES
    pc = LANES
    rr = lax.broadcasted_iota(jnp.int32, (pr, pc), 0)
    cc = lax.broadcasted_iota(jnp.int32, (pr, pc), 1)
    causal = (rr // n_tok == cc // n_tok) & (rr >= cc)
    trow = lax.broadcasted_iota(jnp.int32, (rows, 1), 0)
    tpos = trow % n_tok
    scale = GLA_DK ** -0.5

    def body(p, carry):
        r0 = pl.multiple_of(p * rows, rows)
        r = pl.ds(r0, rows)
        la = _gla_log_decay(a_ref[r, :], wa2_ref, ba_ref)
        big_l = la
        for d in range(1, n_tok):
            big_l = big_l + jnp.where(tpos >= d, pltpu.roll(la, d, axis=0), 0.0)
        l_last = jnp.zeros_like(big_l)
        for j in range(per):
            last_j = big_l[(j + 1) * n_tok - 1:(j + 1) * n_tok, :]
            l_last = jnp.where(trow // n_tok == j, last_j, l_last)
        e_pos = jnp.exp(big_l)
        e_neg = jnp.exp(-big_l)
        e_dec = jnp.exp(l_last - big_l)
        for h in range(GLA_HEADS):
            hs = slice(h * GLA_DK, (h + 1) * GLA_DK)
            vs = slice(h * GLA_DV, (h + 1) * GLA_DV)
            k = k_ref[r, hs]
            q_t = _pad_rows(q_ref[r, hs] * scale * e_pos[:, hs], pr).astype(BF16)
            k_t = _pad_rows(k * e_neg[:, hs], pc).astype(BF16)
            k_dec = k * e_dec[:, hs]
            v = _pad_rows(v_ref[r, vs], pc).astype(BF16)
            attn = jnp.where(causal, _dot_nt(q_t, k_t), 0.0).astype(BF16)
            o = jnp.dot(attn, v, preferred_element_type=F32)[0:rows]
            for j in range(per):
                seq = p * per + j
                in_seq = trow // n_tok == j
                s_old = s0_ref[seq, h]
                o_inter = jnp.dot(q_t, s_old.astype(BF16), preferred_element_type=F32)[0:rows]
                o = o + jnp.where(in_seq, o_inter, 0.0)
                e_last = jnp.exp(big_l[(j + 1) * n_tok - 1:(j + 1) * n_tok, hs])
                k_dec_j = _pad_rows(jnp.where(in_seq, k_dec, 0.0), pc).astype(BF16)
                st_ref[seq, h] = s_old * _col_bcast(e_last, GLA_DV) + _dot_tn(k_dec_j, v)
            y_ref[r, vs] = _gla_gate_norm(o, g_ref[r, vs], on_ref[...])
        return carry

    lax.fori_loop(0, n_groups, body, 0)


def gla_sample(z2, s0, wa2, ba, onorm, *, n_tok, tb):
    m = z2.shape[0]
    b = m // n_tok
    hk = GLA_HEADS * GLA_DK
    hv = GLA_HEADS * GLA_DV
    tr = tb * n_tok
    st_spec = pl.BlockSpec((tb, GLA_HEADS, GLA_DK, GLA_DV), lambda i: (i, 0, 0, 0))
    return pl.pallas_call(
        functools.partial(_gla_sample_kernel, n_tok=n_tok, n_groups=tr // SUBLANES),
        grid=(b // tb,),
        in_specs=[pl.BlockSpec((tr, hk), lambda i: (i, Z_GQ // hk)),
                  pl.BlockSpec((tr, hk), lambda i: (i, Z_GK // hk)),
                  pl.BlockSpec((tr, hv), lambda i: (i, Z_GV // hv)),
                  pl.BlockSpec((tr, hv), lambda i: (i, Z_GG // hv)),
                  pl.BlockSpec((tr, LANES), lambda i: (i, Z_MISC // LANES)),
                  pl.BlockSpec((LANES, hk), lambda i: (0, 0)),
                  pl.BlockSpec((1, hk), lambda i: (0, 0)),
                  pl.BlockSpec((1, GLA_DV), lambda i: (0, 0)),
                  st_spec],
        out_specs=[pl.BlockSpec((tr, hv), lambda i: (i, 0)), st_spec],
        out_shape=[jax.ShapeDtypeStruct((m, hv), F32),
                   jax.ShapeDtypeStruct(s0.shape, F32)],
        compiler_params=_cparams("parallel"),
        name="gla_sample",
    )(z2, z2, z2, z2, z2, wa2, ba.reshape(1, hk), onorm.reshape(1, GLA_DV), s0)


def _rope_tables(pos, reps):
    half = MLA_ROPE // 2
    inv_freq = ROPE_THETA ** (-np.arange(half, dtype=np.float64) / half)
    ang = np.asarray(pos, np.float64)[:, None] * inv_freq[None, :]
    cos = np.concatenate([np.cos(ang), np.cos(ang)], axis=1)
    sin = np.concatenate([-np.sin(ang), np.sin(ang)], axis=1)
    return (jnp.asarray(np.tile(cos, (1, reps)), F32), jnp.asarray(np.tile(sin, (1, reps)), F32))


def _rope(x, cos, sin):
    n = x.shape[1]
    half = MLA_ROPE // 2
    lane = lax.broadcasted_iota(jnp.int32, x.shape, 1)
    partner = jnp.where(lane % MLA_ROPE < half, pltpu.roll(x, n - half, axis=1), pltpu.roll(x, half, axis=1))
    return x * cos + partner * sin


def _mla_prep_kernel(mq_ref, mkv_ref, misc_ref, cos_ref, sin_ref, qn_ref, kvn_ref, wuq_ref, wuk_ref,
                     q_out, kv32_out, kvb_out, cq_ref, q_ref):
    tm = mq_ref.shape[0]
    _norm_rows_to(cq_ref, mq_ref, qn_ref, tm)
    q_ref[...] = jnp.dot(cq_ref[...], wuq_ref[...], preferred_element_type=F32)
    n_nope = MLA_HEADS * MLA_NOPE
    q_rope = _rope(q_ref[:, n_nope:], cos_ref[...], sin_ref[...])
    for h in range(MLA_HEADS):
        q_nope = q_ref[:, h * MLA_NOPE:(h + 1) * MLA_NOPE].astype(BF16)
        q_out[h, :, 0:MLA_KV_RANK] = jnp.dot(q_nope, wuk_ref[h], preferred_element_type=F32).astype(BF16)
        q_out[h, :, MLA_KV_RANK:MLA_QK] = q_rope[:, h * MLA_ROPE:(h + 1) * MLA_ROPE].astype(BF16)
    c = _rms_rows(mkv_ref[...], kvn_ref[...])
    kr = _rope(misc_ref[...], cos_ref[:, 0:LANES], sin_ref[:, 0:LANES])[:, MISC_KR:MISC_KR + MLA_ROPE]
    kv32_out[:, 0:MLA_KV_RANK] = c
    kv32_out[:, MLA_KV_RANK:MLA_QK] = kr
    kvb_out[:, 0:MLA_KV_RANK] = c.astype(BF16)
    kvb_out[:, MLA_KV_RANK:MLA_QK] = kr.astype(BF16)


def mla_prep(z3, cos, sin, q_norm, kv_norm, wuq, wuk_t, *, tm):
    b, s, _ = z3.shape
    nq = MLA_HEADS * (MLA_NOPE + MLA_ROPE)
    nr = MLA_HEADS * MLA_ROPE
    const2 = lambda bi, i: (0, 0)
    return pl.pallas_call(
        _mla_prep_kernel,
        grid=(b, s // tm),
        in_specs=[pl.BlockSpec((None, tm, MLA_Q_RANK), lambda bi, i: (bi, i, Z_MQ // MLA_Q_RANK)),
                  pl.BlockSpec((None, tm, MLA_KV_RANK), lambda bi, i: (bi, i, Z_MKV // MLA_KV_RANK)),
                  pl.BlockSpec((None, tm, LANES), lambda bi, i: (bi, i, Z_MISC // LANES)),
                  pl.BlockSpec((tm, nr), lambda bi, i: (i, 0)),
                  pl.BlockSpec((tm, nr), lambda bi, i: (i, 0)),
                  pl.BlockSpec((1, MLA_Q_RANK), const2),
                  pl.BlockSpec((1, MLA_KV_RANK), const2),
                  pl.BlockSpec((MLA_Q_RANK, nq), const2),
                  pl.BlockSpec((MLA_HEADS, MLA_NOPE, MLA_KV_RANK), lambda bi, i: (0, 0, 0))],
        out_specs=[pl.BlockSpec((None, MLA_HEADS, tm, MLA_QK), lambda bi, i: (bi, 0, i, 0)),
                   pl.BlockSpec((None, tm, MLA_QK), lambda bi, i: (bi, i, 0)),
                   pl.BlockSpec((None, tm, MLA_QK), lambda bi, i: (bi, i, 0))],
        out_shape=[jax.ShapeDtypeStruct((b, MLA_HEADS, s, MLA_QK), BF16),
                   jax.ShapeDtypeStruct((b, s, MLA_QK), F32),
                   jax.ShapeDtypeStruct((b, s, MLA_QK), BF16)],
        scratch_shapes=[pltpu.VMEM((tm, MLA_Q_RANK), BF16), pltpu.VMEM((tm, nq), F32)],
        compiler_params=_cparams("parallel", "parallel"),
        name="mla_prep",
    )(z3, z3, z3, cos, sin, q_norm.reshape(1, -1), kv_norm.reshape(1, -1), wuq, wuk_t)


def _softmax_step(s, kv_c, m_ref, l_ref, acc_ref):
    m_prev = m_ref[...]
    m_new = jnp.maximum(m_prev, jnp.max(s, axis=-1, keepdims=True))
    corr = jnp.exp(m_prev - m_new)
    p = jnp.exp(s - m_new)
    l_ref[...] = l_ref[...] * corr + jnp.sum(p, axis=-1, keepdims=True)
    acc_ref[...] = acc_ref[...] * corr + jnp.dot(p.astype(BF16), kv_c, preferred_element_type=F32)
    m_ref[...] = m_new


def _mla_prompt_kernel(q_ref, kv_ref, wuv_ref, y_ref, m_ref, l_ref, acc_ref, *, tq, tk):
    i = pl.program_id(1)
    rows = MLA_HEADS * tq
    q = q_ref[...].reshape(rows, MLA_QK)
    m_ref[...] = jnp.full_like(m_ref, NEG_INF)
    l_ref[...] = jnp.zeros_like(l_ref)
    acc_ref[...] = jnp.zeros_like(acc_ref)

    def scores(j):
        kv = kv_ref[pl.ds(pl.multiple_of(j * tk, tk), tk), :]
        return _dot_nt(q, kv) * MLA_SCALE, kv[:, 0:MLA_KV_RANK]

    def full_chunk(j, carry):
        s, kv_c = scores(j)
        _softmax_step(s, kv_c, m_ref, l_ref, acc_ref)
        return carry

    last = (i * tq + tq - 1) // tk
    lax.fori_loop(0, last, full_chunk, 0)
    s, kv_c = scores(last)
    qpos = i * tq + lax.broadcasted_iota(jnp.int32, s.shape, 0) % tq
    kpos = last * tk + lax.broadcasted_iota(jnp.int32, s.shape, 1)
    _softmax_step(jnp.where(kpos <= qpos, s, NEG_INF), kv_c, m_ref, l_ref, acc_ref)

    o = (acc_ref[...] / l_ref[...]).astype(BF16)
    for h in range(MLA_HEADS):
        y_ref[:, h * MLA_V:(h + 1) * MLA_V] = jnp.dot(
            o[h * tq:(h + 1) * tq, :], wuv_ref[h], preferred_element_type=F32).astype(y_ref.dtype)


def mla_prompt_attend(q, kvb, wuv_t, *, tq, tk):
    b, _, s, _ = q.shape
    rows = MLA_HEADS * tq
    return pl.pallas_call(
        functools.partial(_mla_prompt_kernel, tq=tq, tk=tk),
        grid=(b, s // tq),
        in_specs=[pl.BlockSpec((None, MLA_HEADS, tq, MLA_QK), lambda bi, i: (bi, 0, i, 0)),
                  pl.BlockSpec((None, s, MLA_QK), lambda bi, i: (bi, 0, 0)),
                  pl.BlockSpec((MLA_HEADS, MLA_KV_RANK, MLA_V), lambda bi, i: (0, 0, 0))],
        out_specs=pl.BlockSpec((None, tq, BRANCH_DIM), lambda bi, i: (bi, i, 0)),
        out_shape=jax.ShapeDtypeStruct((b, s, BRANCH_DIM), BF16),
        scratch_shapes=[pltpu.VMEM((rows, 1), F32), pltpu.VMEM((rows, 1), F32),
                        pltpu.VMEM((rows, MLA_KV_RANK), F32)],
        compiler_params=_cparams("parallel", "arbitrary"),
        name="mla_prompt_attn",
    )(q, kvb, wuv_t)


def _online_update(carry, s, kv_c):
    m, l, acc = carry
    m_new = jnp.maximum(m, jnp.max(s, axis=-1, keepdims=True))
    corr = jnp.exp(m - m_new)
    p = jnp.exp(s - m_new)
    acc = acc * corr + jnp.dot(p.astype(BF16), kv_c, preferred_element_type=F32)
    return m_new, l * corr + jnp.sum(p, axis=-1, keepdims=True), acc


def _mla_sample_kernel(pt_ref, q_ref, new_ref, cache_ref, o_ref, kbuf, sem, *, layer, n_tok, n_chunks, cp):
    b = pl.program_id(0)
    nb = pl.num_programs(0)

    def page_copy(bi, c, p, slot):
        page = pt_ref[bi, c * cp + p]
        return pltpu.make_async_copy(cache_ref.at[layer, page],
                                     kbuf.at[slot, pl.ds(p * PAGE_SIZE, PAGE_SIZE)], sem.at[slot])

    def start_chunk(bi, c, slot):
        for p in range(cp):
            page_copy(bi, c, p, slot).start()

    def wait_chunk(bi, c, slot):
        for p in range(cp):
            page_copy(bi, c, p, slot).wait()

    @pl.when(b == 0)
    def _():
        start_chunk(b, 0, 0)

    q = q_ref[...]
    rows = q.shape[0]
    carry = (jnp.full((rows, 1), NEG_INF, F32), jnp.zeros((rows, 1), F32), jnp.zeros((rows, MLA_KV_RANK), F32))
    for c in range(n_chunks):
        slot = c % 2
        if c + 1 < n_chunks:
            start_chunk(b, c + 1, 1 - slot)
        else:
            @pl.when(b + 1 < nb)
            def _():
                start_chunk(b + 1, 0, 1 - slot)
        wait_chunk(b, c, slot)
        kv = kbuf[slot].astype(BF16)
        s = _dot_nt(q, kv) * MLA_SCALE
        carry = _online_update(carry, s, kv[:, 0:MLA_KV_RANK])
    kv = new_ref[...]
    s = _dot_nt(q, kv) * MLA_SCALE
    tpos = lax.broadcasted_iota(jnp.int32, s.shape, 0) % n_tok
    kpos = lax.broadcasted_iota(jnp.int32, s.shape, 1)
    m, l, acc = _online_update(carry, jnp.where(kpos <= tpos, s, NEG_INF), kv[:, 0:MLA_KV_RANK])
    o_ref[...] = (acc / l).astype(o_ref.dtype)


def mla_sample_attend(q, kv_new_pad, cache, page_table, *, layer, n_tok, cp):
    b, rows, _ = q.shape
    n_pages = page_table.shape[1]
    n_chunks = n_pages // cp
    assert n_chunks % 2 == 0 and n_chunks * cp == n_pages
    grid_spec = pltpu.PrefetchScalarGridSpec(
        num_scalar_prefetch=1,
        grid=(b,),
        in_specs=[pl.BlockSpec((None, rows, MLA_QK), lambda bi, pt: (bi, 0, 0)),
                  pl.BlockSpec((None, PAGE_SIZE, MLA_QK), lambda bi, pt: (bi, 0, 0)),
                  pl.BlockSpec(memory_space=pl.ANY)],
        out_specs=pl.BlockSpec((None, rows, MLA_KV_RANK), lambda bi, pt: (bi, 0, 0)),
        scratch_shapes=[pltpu.VMEM((2, cp * PAGE_SIZE, MLA_QK), F32), pltpu.SemaphoreType.DMA((2,))],
    )
    return pl.pallas_call(
        functools.partial(_mla_sample_kernel, layer=layer, n_tok=n_tok, n_chunks=n_chunks, cp=cp),
        grid_spec=grid_spec,
        out_shape=jax.ShapeDtypeStruct((b, rows, MLA_KV_RANK), BF16),
        compiler_params=_cparams("arbitrary"),
        name="mla_sample_attn",
    )(page_table, q, kv_new_pad, cache)


def _head_mm_kernel(o_ref, w_ref, y_ref):
    y_ref[...] = jnp.dot(o_ref[...], w_ref[...], preferred_element_type=F32).astype(y_ref.dtype)


def head_matmul(o, w):
    h, m, k = o.shape
    n = w.shape[2]
    return pl.pallas_call(
        _head_mm_kernel,
        grid=(h,),
        in_specs=[pl.BlockSpec((None, m, k), lambda i: (i, 0, 0)),
                  pl.BlockSpec((None, k, n), lambda i: (i, 0, 0))],
        out_specs=pl.BlockSpec((m, n), lambda i: (0, i)),
        out_shape=jax.ShapeDtypeStruct((m, h * n), BF16),
        compiler_params=_cparams("parallel"),
        name="mla_value_up",
    )(o, w)


def _branch_mix_kernel(y0_ref, y1_ref, y2_ref, g0_ref, g1_ref, g2_ref, w_ref, o_ref):
    mix = None
    for n, (y_ref, g_ref) in enumerate(((y0_ref, g0_ref), (y1_ref, g1_ref), (y2_ref, g2_ref))):
        proj = jnp.dot(y_ref[...].astype(BF16), w_ref[n], preferred_element_type=F32)
        term = jax.nn.sigmoid(g_ref[...]) * proj
        mix = term if mix is None else mix + term
    o_ref[...] = mix.astype(o_ref.dtype)


def branch_mix(ys, z, w_branch, *, tm, tn):
    m = z.shape[0]
    per = D_MODEL // tn
    yspec = pl.BlockSpec((tm, BRANCH_DIM), lambda i, j: (i, 0))

    def gspec(n):
        return pl.BlockSpec((tm, tn), lambda i, j, n=n: (i, Z_GATES // tn + n * per + j))

    return pl.pallas_call(
        _branch_mix_kernel,
        grid=(m // tm, per),
        in_specs=[yspec, yspec, yspec, gspec(0), gspec(1), gspec(2),
                  pl.BlockSpec((N_BRANCH, BRANCH_DIM, tn), lambda i, j: (0, 0, j))],
        out_specs=pl.BlockSpec((tm, tn), lambda i, j: (i, j)),
        out_shape=jax.ShapeDtypeStruct((m, D_MODEL), BF16),
        compiler_params=_cparams("parallel", "arbitrary"),
        name="branch_mix",
    )(*ys, z, z, z, w_branch)


def _add_norm_rows(o_ref, x_ref, f_ref, g_ref, rows, chunk=32):
    def body(c, carry):
        r = pl.ds(pl.multiple_of(c * chunk, chunk), chunk)
        o_ref[r, :] = x_ref[r, :] + _rms_rows(f_ref[r, :], g_ref[...])
        return carry
    lax.fori_loop(0, rows // chunk, body, 0)


def _out_proj_kernel(mix_ref, w_ref, x_ref, g_ref, o_ref, f_ref):
    f_ref[...] = jnp.dot(mix_ref[...], w_ref[...], preferred_element_type=F32)
    _add_norm_rows(o_ref, x_ref, f_ref, g_ref, x_ref.shape[0])


def out_proj_residual(mix, w_out, x, g, *, tm):
    m, d = x.shape
    return pl.pallas_call(
        _out_proj_kernel,
        grid=(m // tm,),
        in_specs=[pl.BlockSpec((tm, d), lambda i: (i, 0)),
                  pl.BlockSpec((d, d), lambda i: (0, 0)),
                  pl.BlockSpec((tm, d), lambda i: (i, 0)),
                  pl.BlockSpec((1, d), lambda i: (0, 0))],
        out_specs=pl.BlockSpec((tm, d), lambda i: (i, 0)),
        out_shape=jax.ShapeDtypeStruct((m, d), F32),
        scratch_shapes=[pltpu.VMEM((tm, d), F32)],
        compiler_params=_cparams("parallel"),
        name="out_proj",
    )(mix, w_out, x, g.reshape(1, d))


def _ffn_kernel(x_ref, g_in_ref, wg_ref, wu_ref, wd_ref, g_out_ref, o_ref, xn_ref, acc_ref):
    j = pl.program_id(1)

    @pl.when(j == 0)
    def _():
        _norm_rows_to(xn_ref, x_ref, g_in_ref, x_ref.shape[0])
        acc_ref[...] = jnp.zeros_like(acc_ref)

    xn = xn_ref[...]
    gate = jnp.dot(xn, wg_ref[...], preferred_element_type=F32)
    up = jnp.dot(xn, wu_ref[...], preferred_element_type=F32)
    act = (gate * jax.nn.sigmoid(gate) * up).astype(BF16)
    acc_ref[...] += jnp.dot(act, wd_ref[...], preferred_element_type=F32)

    @pl.when(j == pl.num_programs(1) - 1)
    def _():
        _add_norm_rows(o_ref, x_ref, acc_ref, g_out_ref, x_ref.shape[0])


def ffn(x, g_in, w_gu, w_down, g_out, *, tm, tf):
    m, d = x.shape
    nf = D_FF // tf
    return pl.pallas_call(
        _ffn_kernel,
        grid=(m // tm, nf),
        in_specs=[pl.BlockSpec((tm, d), lambda i, j: (i, 0)),
                  pl.BlockSpec((1, d), lambda i, j: (0, 0)),
                  pl.BlockSpec((d, tf), lambda i, j: (0, j)),
                  pl.BlockSpec((d, tf), lambda i, j: (0, j + nf)),
                  pl.BlockSpec((tf, d), lambda i, j: (j, 0)),
                  pl.BlockSpec((1, d), lambda i, j: (0, 0))],
        out_specs=pl.BlockSpec((tm, d), lambda i, j: (i, 0)),
        out_shape=jax.ShapeDtypeStruct((m, d), F32),
        scratch_shapes=[pltpu.VMEM((tm, d), BF16), pltpu.VMEM((tm, d), F32)],
        compiler_params=_cparams("parallel", "arbitrary"),
        name="ffn",
    )(x, g_in.reshape(1, d), w_gu, w_gu, w_down, g_out.reshape(1, d))


def _prep_w_in(w_in):
    splits = (CONV_DIM, CONV_DIM, CONV_DIM,
              GLA_HEADS * GLA_DK, GLA_HEADS * GLA_DK, GLA_HEADS * GLA_DV, GLA_LOWRANK, GLA_HEADS * GLA_DV,
              MLA_Q_RANK, MLA_KV_RANK, MLA_ROPE, N_BRANCH * D_MODEL)
    idx = np.cumsum(splits)[:-1].tolist()
    cb, cc, ch, gq, gk, gv, ga, gg, mq, mkv, mkr, gates = jnp.split(w_in, idx, axis=1)
    d = w_in.shape[0]
    misc_pad = jnp.zeros((d, LANES - MLA_ROPE - GLA_LOWRANK), w_in.dtype)
    tail = jnp.zeros((d, NZ - Z_END), w_in.dtype)
    return jnp.concatenate([gates, cb, cc, ch, gq, gk, gv, gg, mq, mkv, mkr, ga, misc_pad, tail],
                           axis=1).astype(BF16)


def _prep_wuq(wuq):
    w = wuq.reshape(MLA_Q_RANK, MLA_HEADS, MLA_NOPE + MLA_ROPE)
    nope = w[:, :, :MLA_NOPE].reshape(MLA_Q_RANK, MLA_HEADS * MLA_NOPE)
    rope = w[:, :, MLA_NOPE:].reshape(MLA_Q_RANK, MLA_HEADS * MLA_ROPE)
    return jnp.concatenate([nope, rope], axis=1).astype(BF16)


def _layer_weights(l, norms, w_in, conv_w, gla_wa2, gla_ba, gla_onorm, mla_q_norm, mla_kv_norm, mla_wuq,
                   mla_wuk, mla_wuv, w_branch, w_out, ffn_w_gu, ffn_w_down):
    return dict(
        norms=norms[l], w_in=_prep_w_in(w_in[l]), conv_w=conv_w[l],
        wa2=jnp.pad(gla_wa2[l], ((MISC_GA, LANES - MISC_GA - GLA_LOWRANK), (0, 0))).astype(BF16),
        ba=gla_ba[l], onorm=gla_onorm[l],
        q_norm=mla_q_norm[l], kv_norm=mla_kv_norm[l], wuq=_prep_wuq(mla_wuq[l]),
        wuk_t=jnp.transpose(mla_wuk[l], (1, 2, 0)).astype(BF16),
        wuv_t=jnp.transpose(mla_wuv[l], (1, 0, 2)).astype(BF16),
        w_branch=w_branch[l].astype(BF16), w_out=w_out[l].astype(BF16),
        w_gu=ffn_w_gu[l].astype(BF16), w_down=ffn_w_down[l].astype(BF16))


def _finish_layer(x2, z, ys, p, *, tm):
    mix = branch_mix(ys, z, p["w_branch"], tm=tm, tn=512)
    x2 = out_proj_residual(mix, p["w_out"], x2, p["norms"][1], tm=min(tm, 256))
    return ffn(x2, p["norms"][2], p["w_gu"], p["w_down"], p["norms"][3], tm=tm, tf=512)


def _prompt_layer(x2, bsz, seq, rope_tab, p):
    z = norm_matmul(x2, p["norms"][0], p["w_in"], tm=512, tn=NZ_TILE)
    z3 = z.reshape(bsz, seq, NZ)
    y_conv, conv_new = conv_prompt(z3, p["conv_w"], ts=256, tc=512)
    y_gla, gla_new = gla_prompt(z3, p["wa2"], p["ba"], p["onorm"], ts=256)
    q, kv32, kvb = mla_prep(z3, rope_tab[0], rope_tab[1], p["q_norm"], p["kv_norm"], p["wuq"], p["wuk_t"], tm=256)
    y_mla = mla_prompt_attend(q, kvb, p["wuv_t"], tq=128, tk=256)
    ys = [y.reshape(bsz * seq, BRANCH_DIM) for y in (y_conv, y_gla, y_mla)]
    x2 = _finish_layer(x2, z, ys, p, tm=512)
    return x2, kv32, gla_new, conv_new


def _sample_layer(x2, bsz, n_tok, rope_tab, conv_buf, gla_s0, cache, page_table, layer, p):
    m = bsz * n_tok
    z = norm_matmul(x2, p["norms"][0], p["w_in"], tm=m, tn=NZ_TILE)
    y_conv, conv_new = conv_sample(z.reshape(bsz, n_tok * NZ), conv_buf.reshape(bsz, -1), p["conv_w"],
                                   n_tok=n_tok, tb=bsz)
    y_gla, gla_new = gla_sample(z, gla_s0, p["wa2"], p["ba"], p["onorm"], n_tok=n_tok, tb=8)
    q, kv32, kvb = mla_prep(z.reshape(1, m, NZ), rope_tab[0], rope_tab[1], p["q_norm"], p["kv_norm"],
                            p["wuq"], p["wuk_t"], tm=256)
    qs = q.reshape(MLA_HEADS, bsz, n_tok, MLA_QK).transpose(1, 0, 2, 3).reshape(bsz, MLA_HEADS * n_tok, MLA_QK)
    kv_new = jnp.pad(kvb.reshape(bsz, n_tok, MLA_QK), ((0, 0), (0, PAGE_SIZE - n_tok), (0, 0)))
    o_lat = mla_sample_attend(qs, kv_new, cache, page_table, layer=layer, n_tok=n_tok, cp=16)
    o_h = o_lat.reshape(bsz, MLA_HEADS, n_tok, MLA_KV_RANK).transpose(1, 0, 2, 3).reshape(MLA_HEADS, m, MLA_KV_RANK)
    y_mla = head_matmul(o_h, p["wuv_t"])
    ys = [y_conv.reshape(m, BRANCH_DIM), y_gla, y_mla]
    x2 = _finish_layer(x2, z, ys, p, tm=m)
    return (x2, kv32.reshape(bsz, n_tok, MLA_QK), gla_new,
            conv_new.reshape(bsz, CONV_WIDTH - 1, CONV_DIM))


def kernel(x_prompt, x_sample, cache_mla, page_table, state_gla, state_conv, norms, w_in, conv_w, gla_wa2, gla_ba, gla_onorm, mla_q_norm, mla_kv_norm, mla_wuq, mla_wuk, mla_wuv, w_branch, w_out, ffn_w_gu, ffn_w_down):
    bp, sp, d = x_prompt.shape
    bs, ss, _ = x_sample.shape
    depth = norms.shape[0]
    past_len = page_table.shape[1] * PAGE_SIZE
    tab_p = _rope_tables(np.arange(sp), MLA_HEADS)
    tab_s = _rope_tables(past_len + np.arange(bs * ss) % ss, MLA_HEADS)
    xp = x_prompt.reshape(bp * sp, d)
    xs = x_sample.reshape(bs * ss, d)
    outs = [[] for _ in range(6)]
    for l in range(depth):
        p = _layer_weights(l, norms, w_in, conv_w, gla_wa2, gla_ba, gla_onorm, mla_q_norm, mla_kv_norm, mla_wuq,
                           mla_wuk, mla_wuv, w_branch, w_out, ffn_w_gu, ffn_w_down)
        xp, kv_p, gla_p, conv_p = _prompt_layer(xp, bp, sp, tab_p, p)
        xs, kv_s, gla_s, conv_s = _sample_layer(xs, bs, ss, tab_s, state_conv[l], state_gla[l], cache_mla,
                                                page_table, l, p)
        for acc, val in zip(outs, (kv_p, kv_s, gla_p, gla_s, conv_p, conv_s)):
            acc.append(val)
    return (xp.reshape(bp, sp, d), xs.reshape(bs, ss, d)) + tuple(jnp.stack(o) for o in outs)
```

```python
import functools
import math

import numpy as np
import jax
import jax.numpy as jnp
from jax import lax
from jax.experimental import pallas as pl
from jax.experimental.pallas import tpu as pltpu

F32 = jnp.float32
BF16 = jnp.bfloat16

D_MODEL = 2048
BRANCH_DIM = 1024
N_BRANCH = 3
CONV_DIM = BRANCH_DIM
CONV_WIDTH = 3
GLA_HEADS = 4
GLA_DK = 128
GLA_DV = BRANCH_DIM // GLA_HEADS
GLA_LOWRANK = 16
GLA_TAU = 16.0
GLA_CHUNK = 64
MLA_HEADS = 8
MLA_Q_RANK = 512
MLA_KV_RANK = 256
MLA_NOPE = 128
MLA_ROPE = 64
MLA_V = BRANCH_DIM // MLA_HEADS
MLA_SCALE = (MLA_NOPE + MLA_ROPE) ** -0.5
MLA_QK = MLA_KV_RANK + MLA_ROPE
ROPE_THETA = 10000.0
PAGE_SIZE = 128
D_FF = -(-8 * D_MODEL // (3 * 256)) * 256
RMS_EPS = 1e-6
NEG_INF = -1e30

LANES = 128
SUBLANES = 8
VMEM_LIMIT_BYTES = 56 * 1024 * 1024

Z_GATES = 0
Z_CONV = Z_GATES + N_BRANCH * D_MODEL
Z_GQ = Z_CONV + 3 * CONV_DIM
Z_GK = Z_GQ + GLA_HEADS * GLA_DK
Z_GV = Z_GK + GLA_HEADS * GLA_DK
Z_GG = Z_GV + GLA_HEADS * GLA_DV
Z_MQ = Z_GG + GLA_HEADS * GLA_DV
Z_MKV = Z_MQ + MLA_Q_RANK
Z_MISC = Z_MKV + MLA_KV_RANK
Z_END = Z_MISC + LANES
NZ_TILE = 512
NZ = -(-Z_END // NZ_TILE) * NZ_TILE
MISC_KR = 0
MISC_GA = MLA_ROPE


def _cparams(*sem):
    return pltpu.CompilerParams(dimension_semantics=sem, vmem_limit_bytes=VMEM_LIMIT_BYTES)


def _rms_rows(x, g):
    ms = jnp.mean(x * x, axis=-1, keepdims=True)
    return x * lax.rsqrt(ms + RMS_EPS) * g


def _norm_rows_to(dst_ref, src_ref, g_ref, rows, chunk=32):
    def body(c, carry):
        r0 = pl.multiple_of(c * chunk, chunk)
        dst_ref[pl.ds(r0, chunk), :] = _rms_rows(src_ref[pl.ds(r0, chunk), :], g_ref[...]).astype(dst_ref.dtype)
        return carry
    lax.fori_loop(0, rows // chunk, body, 0)


def _norm_mm_kernel(x_ref, g_ref, w_ref, o_ref, xn_ref):
    @pl.when(pl.program_id(1) == 0)
    def _():
        _norm_rows_to(xn_ref, x_ref, g_ref, x_ref.shape[0])
    o_ref[...] = jnp.dot(xn_ref[...], w_ref[...], preferred_element_type=F32)


def norm_matmul(x, g, w, *, tm, tn):
    m, k = x.shape
    n = w.shape[1]
    return pl.pallas_call(
        _norm_mm_kernel,
        grid=(m // tm, n // tn),
        in_specs=[pl.BlockSpec((tm, k), lambda i, j: (i, 0)),
                  pl.BlockSpec((1, k), lambda i, j: (0, 0)),
                  pl.BlockSpec((k, tn), lambda i, j: (0, j))],
        out_specs=pl.BlockSpec((tm, tn), lambda i, j: (i, j)),
        out_shape=jax.ShapeDtypeStruct((m, n), F32),
        scratch_shapes=[pltpu.VMEM((tm, k), BF16)],
        compiler_params=_cparams("parallel", "arbitrary"),
        name="in_proj",
    )(x, g.reshape(1, k), w)


def _conv_prompt_kernel(cb_ref, cc_ref, ch_ref, w_ref, y_ref, st_ref, carry_ref):
    i = pl.program_id(2)

    @pl.when(i == 0)
    def _():
        carry_ref[...] = jnp.zeros_like(carry_ref)

    u = cc_ref[...] * ch_ref[...]
    ts = u.shape[0]
    row = lax.broadcasted_iota(jnp.int32, u.shape, 0)
    prev1 = carry_ref[1:2, :]
    prev2 = carry_ref[0:1, :]
    um1 = jnp.where(row == 0, prev1, pltpu.roll(u, 1, axis=0))
    um2 = jnp.where(row == 0, prev2, jnp.where(row == 1, prev1, pltpu.roll(u, 2, axis=0)))
    w = w_ref[...]
    y = um2 * w[0:1, :] + um1 * w[1:2, :] + u * w[2:3, :]
    y_ref[...] = (cb_ref[...] * y).astype(y_ref.dtype)
    last = u[ts - 2:ts, :]
    carry_ref[...] = last

    @pl.when(i == pl.num_programs(2) - 1)
    def _():
        st_ref[...] = last


def conv_prompt(z3, conv_w, *, ts, tc):
    b, s, _ = z3.shape
    c = CONV_DIM
    base = Z_CONV // tc
    nc = c // tc

    def zspec(part):
        return pl.BlockSpec((None, ts, tc), lambda bi, ci, i, part=part: (bi, i, base + part * nc + ci))

    return pl.pallas_call(
        _conv_prompt_kernel,
        grid=(b, nc, s // ts),
        in_specs=[zspec(0), zspec(1), zspec(2),
                  pl.BlockSpec((CONV_WIDTH, tc), lambda bi, ci, i: (0, ci))],
        out_specs=[pl.BlockSpec((None, ts, tc), lambda bi, ci, i: (bi, i, ci)),
                   pl.BlockSpec((None, CONV_WIDTH - 1, tc), lambda bi, ci, i: (bi, 0, ci))],
        out_shape=[jax.ShapeDtypeStruct((b, s, c), BF16),
                   jax.ShapeDtypeStruct((b, CONV_WIDTH - 1, c), F32)],
        scratch_shapes=[pltpu.VMEM((CONV_WIDTH - 1, tc), F32)],
        compiler_params=_cparams("parallel", "parallel", "arbitrary"),
        name="conv_prompt",
    )(z3, z3, z3, conv_w)


def _conv_sample_kernel(*refs, n_tok):
    cb = refs[0:n_tok]
    cc = refs[n_tok:2 * n_tok]
    ch = refs[2 * n_tok:3 * n_tok]
    buf = refs[3 * n_tok:3 * n_tok + 2]
    w_ref = refs[3 * n_tok + 2]
    y_ref, st_ref = refs[3 * n_tok + 3:]
    c = w_ref.shape[1]
    w = w_ref[...]
    full = [buf[0][...], buf[1][...]] + [cc[t][...] * ch[t][...] for t in range(n_tok)]
    for t in range(n_tok):
        y = full[t] * w[0:1, :] + full[t + 1] * w[1:2, :] + full[t + 2] * w[2:3, :]
        y_ref[:, t * c:(t + 1) * c] = cb[t][...] * y
    st_ref[:, 0:c] = full[n_tok]
    st_ref[:, c:2 * c] = full[n_tok + 1]


def conv_sample(z2, buf2, conv_w, *, n_tok, tb):
    b = z2.shape[0]
    c = CONV_DIM
    per_tok = NZ // c
    base = Z_CONV // c

    def zspec(part, t):
        return pl.BlockSpec((tb, c), lambda i, part=part, t=t: (i, t * per_tok + base + part))

    in_specs = [zspec(part, t) for part in range(3) for t in range(n_tok)]
    in_specs += [pl.BlockSpec((tb, c), lambda i, j=j: (i, j)) for j in range(2)]
    in_specs += [pl.BlockSpec((CONV_WIDTH, c), lambda i: (0, 0))]
    return pl.pallas_call(
        functools.partial(_conv_sample_kernel, n_tok=n_tok),
        grid=(b // tb,),
        in_specs=in_specs,
        out_specs=[pl.BlockSpec((tb, n_tok * c), lambda i: (i, 0)),
                   pl.BlockSpec((tb, 2 * c), lambda i: (i, 0))],
        out_shape=[jax.ShapeDtypeStruct((b, n_tok * c), F32),
                   jax.ShapeDtypeStruct((b, 2 * c), F32)],
        compiler_params=_cparams("parallel"),
        name="conv_sample",
    )(*([z2] * (3 * n_tok)), buf2, buf2, conv_w)


def _log_sigmoid(x):
    return jnp.minimum(x, 0.0) - jnp.log1p(jnp.exp(-jnp.abs(x)))


def _split3(x):
    hi = x.astype(BF16)
    r1 = x - hi.astype(F32)
    mid = r1.astype(BF16)
    lo = (r1 - mid.astype(F32)).astype(BF16)
    return hi, mid, lo


def _col_bcast(row, width):
    n = row.shape[1]
    sq = jnp.transpose(jnp.broadcast_to(row, (n, n)))
    return jnp.concatenate([sq] * (width // n), axis=1)


def _dot_nt(a, b):
    return lax.dot_general(a, b, (((1,), (1,)), ((), ())), preferred_element_type=F32)


def _dot_tn(a, b):
    return lax.dot_general(a, b, (((0,), (0,)), ((), ())), preferred_element_type=F32)


def _gla_gate_norm(o, g, onorm):
    on = o * lax.rsqrt(jnp.mean(o * o, axis=-1, keepdims=True) + RMS_EPS) * onorm
    return on * (g * jax.nn.sigmoid(g))


def _pad_rows(x, rows):
    if x.shape[0] == rows:
        return x
    return jnp.concatenate([x, jnp.zeros((rows - x.shape[0], x.shape[1]), x.dtype)], axis=0)


def _gla_log_decay(misc, wa2_ref, ba_ref):
    lane = lax.broadcasted_iota(jnp.int32, misc.shape, 1)
    a_low = jnp.where((lane >= MISC_GA) & (lane < MISC_GA + GLA_LOWRANK), misc, 0.0).astype(BF16)
    a = jnp.dot(a_low, wa2_ref[...], preferred_element_type=F32) + ba_ref[...]
    return _log_sigmoid(a) * (1.0 / GLA_TAU)


def _gla_prompt_kernel(q_ref, k_ref, v_ref, g_ref, a_ref, wa2_ref, ba_ref, on_ref, y_ref, st_ref, s_ref, *, ts):
    i = pl.program_id(1)

    @pl.when(i == 0)
    def _():
        s_ref[...] = jnp.zeros_like(s_ref)

    c = GLA_CHUNK
    pc = LANES
    rr = lax.broadcasted_iota(jnp.int32, (c, pc), 0)
    cc = lax.broadcasted_iota(jnp.int32, (c, pc), 1)
    tril = rr >= cc
    tril_b = tril.astype(BF16)
    scale = GLA_DK ** -0.5
    for ci in range(ts // c):
        r = slice(ci * c, (ci + 1) * c)
        la = _gla_log_decay(a_ref[r, :], wa2_ref, ba_ref)
        big_l = None
        for term in _split3(_pad_rows(la, pc)):
            part = jnp.dot(tril_b, term, preferred_element_type=F32)
            big_l = part if big_l is None else big_l + part
        l_last = big_l[c - 1:c, :]
        e_pos = jnp.exp(big_l)
        e_neg = jnp.exp(-big_l)
        e_dec = jnp.exp(l_last - big_l)
        e_last = jnp.exp(l_last)
        for h in range(GLA_HEADS):
            hs = slice(h * GLA_DK, (h + 1) * GLA_DK)
            vs = slice(h * GLA_DV, (h + 1) * GLA_DV)
            k = k_ref[r, hs]
            q_t = (q_ref[r, hs] * scale * e_pos[:, hs]).astype(BF16)
            k_t = _pad_rows(k * e_neg[:, hs], pc).astype(BF16)
            k_dec = _pad_rows(k * e_dec[:, hs], pc).astype(BF16)
            v = _pad_rows(v_ref[r, vs], pc).astype(BF16)
            attn = jnp.where(tril, _dot_nt(q_t, k_t), 0.0).astype(BF16)
            s_old = s_ref[h]
            o = (jnp.dot(q_t, s_old.astype(BF16), preferred_element_type=F32)
                 + jnp.dot(attn, v, preferred_element_type=F32))
            s_ref[h] = s_old * _col_bcast(e_last[:, hs], GLA_DV) + _dot_tn(k_dec, v)
            y_ref[r, vs] = _gla_gate_norm(o, g_ref[r, vs], on_ref[...]).astype(y_ref.dtype)

    @pl.when(i == pl.num_programs(1) - 1)
    def _():
        st_ref[...] = s_ref[...]


def gla_prompt(z3, wa2, ba, onorm, *, ts):
    b, s, _ = z3.shape
    hk = GLA_HEADS * GLA_DK
    hv = GLA_HEADS * GLA_DV
    return pl.pallas_call(
        functools.partial(_gla_prompt_kernel, ts=ts),
        grid=(b, s // ts),
        in_specs=[pl.BlockSpec((None, ts, hk), lambda bi, i: (bi, i, Z_GQ // hk)),
                  pl.BlockSpec((None, ts, hk), lambda bi, i: (bi, i, Z_GK // hk)),
                  pl.BlockSpec((None, ts, hv), lambda bi, i: (bi, i, Z_GV // hv)),
                  pl.BlockSpec((None, ts, hv), lambda bi, i: (bi, i, Z_GG // hv)),
                  pl.BlockSpec((None, ts, LANES), lambda bi, i: (bi, i, Z_MISC // LANES)),
                  pl.BlockSpec((LANES, hk), lambda bi, i: (0, 0)),
                  pl.BlockSpec((1, hk), lambda bi, i: (0, 0)),
                  pl.BlockSpec((1, GLA_DV), lambda bi, i: (0, 0))],
        out_specs=[pl.BlockSpec((None, ts, hv), lambda bi, i: (bi, i, 0)),
                   pl.BlockSpec((None, GLA_HEADS, GLA_DK, GLA_DV), lambda bi, i: (bi, 0, 0, 0))],
        out_shape=[jax.ShapeDtypeStruct((b, s, hv), BF16),
                   jax.ShapeDtypeStruct((b, GLA_HEADS, GLA_DK, GLA_DV), F32)],
        scratch_shapes=[pltpu.VMEM((GLA_HEADS, GLA_DK, GLA_DV), F32)],
        compiler_params=_cparams("parallel", "arbitrary"),
        name="gla_prompt",
    )(z3, z3, z3, z3, z3, wa2, ba.reshape(1, hk), onorm.reshape(1, GLA_DV))


def _gla_sample_kernel(q_ref, k_ref, v_ref, g_ref, a_ref, wa2_ref, ba_ref, on_ref, s0_ref, y_ref, st_ref,
                       *, n_tok, n_groups):
    rows = SUBLANES
    per = rows // n_tok
    pr = 2 * SUBLANES
    pc = LANES
    rr = lax.broadcasted_iota(jnp.int32, (pr, pc), 0)
    cc = lax.broadcasted_iota(jnp.int32, (pr, pc), 1)
    causal = (rr // n_tok == cc // n_tok) & (rr >= cc)
    trow = lax.broadcasted_iota(jnp.int32, (rows, 1), 0)
    tpos = trow % n_tok
    scale = GLA_DK ** -0.5

    def body(p, carry):
        r0 = pl.multiple_of(p * rows, rows)
        r = pl.ds(r0, rows)
        la = _gla_log_decay(a_ref[r, :], wa2_ref, ba_ref)
        big_l = la
        for d in range(1, n_tok):
            big_l = big_l + jnp.where(tpos >= d, pltpu.roll(la, d, axis=0), 0.0)
        l_last = jnp.zeros_like(big_l)
        for j in range(per):
            last_j = big_l[(j + 1) * n_tok - 1:(j + 1) * n_tok, :]
            l_last = jnp.where(trow // n_tok == j, last_j, l_last)
        e_pos = jnp.exp(big_l)
        e_neg = jnp.exp(-big_l)
        e_dec = jnp.exp(l_last - big_l)
        for h in range(GLA_HEADS):
            hs = slice(h * GLA_DK, (h + 1) * GLA_DK)
            vs = slice(h * GLA_DV, (h + 1) * GLA_DV)
            k = k_ref[r, hs]
            q_t = _pad_rows(q_ref[r, hs] * scale * e_pos[:, hs], pr).astype(BF16)
            k_t = _pad_rows(k * e_neg[:, hs], pc).astype(BF16)
            k_dec = k * e_dec[:, hs]
            v = _pad_rows(v_ref[r, vs], pc).astype(BF16)
            attn = jnp.where(causal, _dot_nt(q_t, k_t), 0.0).astype(BF16)
            o = jnp.dot(attn, v, preferred_element_type=F32)[0:rows]
            for j in range(per):
                seq = p * per + j
                in_seq = trow // n_tok == j
                s_old = s0_ref[seq, h]
                o_inter = jnp.dot(q_t, s_old.astype(BF16), preferred_element_type=F32)[0:rows]
                o = o + jnp.where(in_seq, o_inter, 0.0)
                e_last = jnp.exp(big_l[(j + 1) * n_tok - 1:(j + 1) * n_tok, hs])
                k_dec_j = _pad_rows(jnp.where(in_seq, k_dec, 0.0), pc).astype(BF16)
                st_ref[seq, h] = s_old * _col_bcast(e_last, GLA_DV) + _dot_tn(k_dec_j, v)
            y_ref[r, vs] = _gla_gate_norm(o, g_ref[r, vs], on_ref[...])
        return carry

    lax.fori_loop(0, n_groups, body, 0)


def gla_sample(z2, s0_all, layer, wa2, ba, onorm, *, n_tok, tb):
    m = z2.shape[0]
    b = m // n_tok
    hk = GLA_HEADS * GLA_DK
    hv = GLA_HEADS * GLA_DV
    tr = tb * n_tok
    st_spec = pl.BlockSpec((tb, GLA_HEADS, GLA_DK, GLA_DV), lambda i: (i, 0, 0, 0))
    s0_spec = pl.BlockSpec((None, tb, GLA_HEADS, GLA_DK, GLA_DV), lambda i: (layer, i, 0, 0, 0))
    return pl.pallas_call(
        functools.partial(_gla_sample_kernel, n_tok=n_tok, n_groups=tr // SUBLANES),
        grid=(b // tb,),
        in_specs=[pl.BlockSpec((tr, hk), lambda i: (i, Z_GQ // hk)),
                  pl.BlockSpec((tr, hk), lambda i: (i, Z_GK // hk)),
                  pl.BlockSpec((tr, hv), lambda i: (i, Z_GV // hv)),
                  pl.BlockSpec((tr, hv), lambda i: (i, Z_GG // hv)),
                  pl.BlockSpec((tr, LANES), lambda i: (i, Z_MISC // LANES)),
                  pl.BlockSpec((LANES, hk), lambda i: (0, 0)),
                  pl.BlockSpec((1, hk), lambda i: (0, 0)),
                  pl.BlockSpec((1, GLA_DV), lambda i: (0, 0)),
                  s0_spec],
        out_specs=[pl.BlockSpec((tr, hv), lambda i: (i, 0)), st_spec],
        out_shape=[jax.ShapeDtypeStruct((m, hv), F32),
                   jax.ShapeDtypeStruct(s0_all.shape[1:], F32)],
        compiler_params=_cparams("parallel"),
        name="gla_sample",
    )(z2, z2, z2, z2, z2, wa2, ba.reshape(1, hk), onorm.reshape(1, GLA_DV), s0_all)


def _rope_tables(pos, reps):
    half = MLA_ROPE // 2
    inv_freq = ROPE_THETA ** (-np.arange(half, dtype=np.float64) / half)
    ang = np.asarray(pos, np.float64)[:, None] * inv_freq[None, :]
    cos = np.concatenate([np.cos(ang), np.cos(ang)], axis=1)
    sin = np.concatenate([-np.sin(ang), np.sin(ang)], axis=1)
    return (jnp.asarray(np.tile(cos, (1, reps)), F32), jnp.asarray(np.tile(sin, (1, reps)), F32))


def _rope(x, cos, sin):
    n = x.shape[1]
    half = MLA_ROPE // 2
    lane = lax.broadcasted_iota(jnp.int32, x.shape, 1)
    partner = jnp.where(lane % MLA_ROPE < half, pltpu.roll(x, n - half, axis=1), pltpu.roll(x, half, axis=1))
    return x * cos + partner * sin


def _mla_prep_kernel(mq_ref, mkv_ref, misc_ref, cos_ref, sin_ref, qn_ref, kvn_ref, wuq_ref, wuk_ref,
                     q_out, kv32_out, kvb_out, cq_ref, q_ref):
    tm = mq_ref.shape[0]
    _norm_rows_to(cq_ref, mq_ref, qn_ref, tm)
    q_ref[...] = jnp.dot(cq_ref[...], wuq_ref[...], preferred_element_type=F32)
    n_nope = MLA_HEADS * MLA_NOPE
    q_rope = _rope(q_ref[:, n_nope:], cos_ref[...], sin_ref[...])
    for h in range(MLA_HEADS):
        q_nope = q_ref[:, h * MLA_NOPE:(h + 1) * MLA_NOPE].astype(BF16)
        q_out[h, :, 0:MLA_KV_RANK] = jnp.dot(q_nope, wuk_ref[h], preferred_element_type=F32).astype(BF16)
        q_out[h, :, MLA_KV_RANK:MLA_QK] = q_rope[:, h * MLA_ROPE:(h + 1) * MLA_ROPE].astype(BF16)
    c = _rms_rows(mkv_ref[...], kvn_ref[...])
    kr = _rope(misc_ref[...], cos_ref[:, 0:LANES], sin_ref[:, 0:LANES])[:, MISC_KR:MISC_KR + MLA_ROPE]
    kv32_out[:, 0:MLA_KV_RANK] = c
    kv32_out[:, MLA_KV_RANK:MLA_QK] = kr
    kvb_out[:, 0:MLA_KV_RANK] = c.astype(BF16)
    kvb_out[:, MLA_KV_RANK:MLA_QK] = kr.astype(BF16)


def mla_prep(z3, cos, sin, q_norm, kv_norm, wuq, wuk_t, *, tm):
    b, s, _ = z3.shape
    nq = MLA_HEADS * (MLA_NOPE + MLA_ROPE)
    nr = MLA_HEADS * MLA_ROPE
    const2 = lambda bi, i: (0, 0)
    return pl.pallas_call(
        _mla_prep_kernel,
        grid=(b, s // tm),
        in_specs=[pl.BlockSpec((None, tm, MLA_Q_RANK), lambda bi, i: (bi, i, Z_MQ // MLA_Q_RANK)),
                  pl.BlockSpec((None, tm, MLA_KV_RANK), lambda bi, i: (bi, i, Z_MKV // MLA_KV_RANK)),
                  pl.BlockSpec((None, tm, LANES), lambda bi, i: (bi, i, Z_MISC // LANES)),
                  pl.BlockSpec((tm, nr), lambda bi, i: (i, 0)),
                  pl.BlockSpec((tm, nr), lambda bi, i: (i, 0)),
                  pl.BlockSpec((1, MLA_Q_RANK), const2),
                  pl.BlockSpec((1, MLA_KV_RANK), const2),
                  pl.BlockSpec((MLA_Q_RANK, nq), const2),
                  pl.BlockSpec((MLA_HEADS, MLA_NOPE, MLA_KV_RANK), lambda bi, i: (0, 0, 0))],
        out_specs=[pl.BlockSpec((None, MLA_HEADS, tm, MLA_QK), lambda bi, i: (bi, 0, i, 0)),
                   pl.BlockSpec((None, tm, MLA_QK), lambda bi, i: (bi, i, 0)),
                   pl.BlockSpec((None, tm, MLA_QK), lambda bi, i: (bi, i, 0))],
        out_shape=[jax.ShapeDtypeStruct((b, MLA_HEADS, s, MLA_QK), BF16),
                   jax.ShapeDtypeStruct((b, s, MLA_QK), F32),
                   jax.ShapeDtypeStruct((b, s, MLA_QK), BF16)],
        scratch_shapes=[pltpu.VMEM((tm, MLA_Q_RANK), BF16), pltpu.VMEM((tm, nq), F32)],
        compiler_params=_cparams("parallel", "parallel"),
        name="mla_prep",
    )(z3, z3, z3, cos, sin, q_norm.reshape(1, -1), kv_norm.reshape(1, -1), wuq, wuk_t)


SOFTMAX_EXP2_SCALE = MLA_SCALE * math.log2(math.e)


def _online_softmax(s, kv_c_t, m_prev, l_prev, acc_prev, contract_kv_lanes):
    m_new = jnp.maximum(m_prev, jnp.max(s, axis=-1, keepdims=True))
    corr = jnp.exp2((m_prev - m_new) * SOFTMAX_EXP2_SCALE)
    p = jnp.exp2((s - m_new) * SOFTMAX_EXP2_SCALE)
    pv = _dot_nt(p.astype(BF16), kv_c_t) if contract_kv_lanes else jnp.dot(
        p.astype(BF16), kv_c_t, preferred_element_type=F32)
    return m_new, l_prev * corr + jnp.sum(p, axis=-1, keepdims=True), acc_prev * corr + pv


def _mla_prompt_kernel(q_ref, kv_ref, wuv_ref, y_ref, m_ref, l_ref, acc_ref, *, tq, tk, hb):
    i = pl.program_id(1)
    rb = hb * tq
    m_ref[...] = jnp.full_like(m_ref, NEG_INF)
    l_ref[...] = jnp.zeros_like(l_ref)
    acc_ref[...] = jnp.zeros_like(acc_ref)

    def chunk(j, masked):
        kv = kv_ref[pl.ds(pl.multiple_of(j * tk, tk), tk), :]
        kv_c = kv[:, 0:MLA_KV_RANK]
        for r in range(MLA_HEADS // hb):
            rs = slice(r * rb, (r + 1) * rb)
            s = _dot_nt(q_ref[r * hb:(r + 1) * hb].reshape(rb, MLA_QK), kv)
            if masked:
                qpos = i * tq + lax.broadcasted_iota(jnp.int32, s.shape, 0) % tq
                kpos = j * tk + lax.broadcasted_iota(jnp.int32, s.shape, 1)
                s = jnp.where(kpos <= qpos, s, NEG_INF)
            m_ref[rs], l_ref[rs], acc_ref[rs] = _online_softmax(s, kv_c, m_ref[rs], l_ref[rs], acc_ref[rs], False)

    def full_chunk(j, carry):
        chunk(j, False)
        return carry

    last = (i * tq + tq - 1) // tk
    lax.fori_loop(0, last, full_chunk, 0)
    chunk(last, True)

    for h in range(MLA_HEADS):
        hs = slice(h * tq, (h + 1) * tq)
        o = (acc_ref[hs] / l_ref[hs]).astype(BF16)
        y_ref[:, h * MLA_V:(h + 1) * MLA_V] = jnp.dot(o, wuv_ref[h], preferred_element_type=F32).astype(y_ref.dtype)


def mla_prompt_attend(q, kvb, wuv_t, *, tq, tk, hb):
    b, _, s, _ = q.shape
    rows = MLA_HEADS * tq
    return pl.pallas_call(
        functools.partial(_mla_prompt_kernel, tq=tq, tk=tk, hb=hb),
        grid=(b, s // tq),
        in_specs=[pl.BlockSpec((None, MLA_HEADS, tq, MLA_QK), lambda bi, i: (bi, 0, i, 0)),
                  pl.BlockSpec((None, s, MLA_QK), lambda bi, i: (bi, 0, 0)),
                  pl.BlockSpec((MLA_HEADS, MLA_KV_RANK, MLA_V), lambda bi, i: (0, 0, 0))],
        out_specs=pl.BlockSpec((None, tq, BRANCH_DIM), lambda bi, i: (bi, i, 0)),
        out_shape=jax.ShapeDtypeStruct((b, s, BRANCH_DIM), BF16),
        scratch_shapes=[pltpu.VMEM((rows, 1), F32), pltpu.VMEM((rows, 1), F32),
                        pltpu.VMEM((rows, MLA_KV_RANK), F32)],
        compiler_params=_cparams("parallel", "arbitrary"),
        name="mla_prompt_attn",
    )(q, kvb, wuv_t)


def _mla_sample_kernel(pt_ref, q_ref, new_ref, cache_ref, o_ref, kbuf, sem, *, layer, n_tok, n_chunks, cp):
    b = pl.program_id(0)
    nb = pl.num_programs(0)

    def page_copy(bi, c, p, slot):
        page = pt_ref[bi, c * cp + p]
        return pltpu.make_async_copy(cache_ref.at[layer, page], kbuf.at[slot, p], sem.at[slot])

    def start_chunk(bi, c, slot):
        for p in range(cp):
            page_copy(bi, c, p, slot).start()

    def wait_chunk(bi, c, slot):
        for p in range(cp):
            page_copy(bi, c, p, slot).wait()

    @pl.when(b == 0)
    def _():
        start_chunk(b, 0, 0)

    q = q_ref[...]
    rows = q.shape[0]

    def consume(slot, carry):
        kt = jnp.concatenate([kbuf[slot, p].astype(BF16) for p in range(cp)], axis=1)
        s = jnp.dot(q, kt, preferred_element_type=F32)
        return _online_softmax(s, kt[0:MLA_KV_RANK, :], *carry, True)

    def chunk_pair(c2, carry):
        c0 = 2 * c2
        start_chunk(b, c0 + 1, 1)
        wait_chunk(b, c0, 0)
        carry = consume(0, carry)

        @pl.when(c0 + 2 < n_chunks)
        def _():
            start_chunk(b, c0 + 2, 0)

        @pl.when((c0 + 2 >= n_chunks) & (b + 1 < nb))
        def _():
            start_chunk(b + 1, 0, 0)

        wait_chunk(b, c0 + 1, 1)
        return consume(1, carry)

    carry = (jnp.full((rows, 1), NEG_INF, F32), jnp.zeros((rows, 1), F32), jnp.zeros((rows, MLA_KV_RANK), F32))
    carry = lax.fori_loop(0, n_chunks // 2, chunk_pair, carry)
    kv = new_ref[...]
    s = _dot_nt(q, kv)
    tpos = lax.broadcasted_iota(jnp.int32, s.shape, 0) % n_tok
    kpos = lax.broadcasted_iota(jnp.int32, s.shape, 1)
    m, l, acc = _online_softmax(jnp.where(kpos <= tpos, s, NEG_INF), kv[:, 0:MLA_KV_RANK], *carry, False)
    o_ref[...] = (acc / l).astype(o_ref.dtype)


def mla_sample_attend(q, kv_new_pad, cache_t, page_table, *, layer, n_tok, cp):
    b, rows, _ = q.shape
    n_pages = page_table.shape[1]
    n_chunks = n_pages // cp
    assert n_chunks % 2 == 0 and n_chunks * cp == n_pages
    grid_spec = pltpu.PrefetchScalarGridSpec(
        num_scalar_prefetch=1,
        grid=(b,),
        in_specs=[pl.BlockSpec((None, rows, MLA_QK), lambda bi, pt: (bi, 0, 0)),
                  pl.BlockSpec((None, PAGE_SIZE, MLA_QK), lambda bi, pt: (bi, 0, 0)),
                  pl.BlockSpec(memory_space=pl.ANY)],
        out_specs=pl.BlockSpec((None, rows, MLA_KV_RANK), lambda bi, pt: (bi, 0, 0)),
        scratch_shapes=[pltpu.VMEM((2, cp, MLA_QK, PAGE_SIZE), F32), pltpu.SemaphoreType.DMA((2,))],
    )
    return pl.pallas_call(
        functools.partial(_mla_sample_kernel, layer=layer, n_tok=n_tok, n_chunks=n_chunks, cp=cp),
        grid_spec=grid_spec,
        out_shape=jax.ShapeDtypeStruct((b, rows, MLA_KV_RANK), BF16),
        compiler_params=_cparams("arbitrary"),
        name="mla_sample_attn",
    )(page_table, q, kv_new_pad, cache_t)


def _head_mm_kernel(o_ref, w_ref, y_ref):
    y_ref[...] = jnp.dot(o_ref[...], w_ref[...], preferred_element_type=F32).astype(y_ref.dtype)


def head_matmul(o, w):
    h, m, k = o.shape
    n = w.shape[2]
    return pl.pallas_call(
        _head_mm_kernel,
        grid=(h,),
        in_specs=[pl.BlockSpec((None, m, k), lambda i: (i, 0, 0)),
                  pl.BlockSpec((None, k, n), lambda i: (i, 0, 0))],
        out_specs=pl.BlockSpec((m, n), lambda i: (0, i)),
        out_shape=jax.ShapeDtypeStruct((m, h * n), BF16),
        compiler_params=_cparams("parallel"),
        name="mla_value_up",
    )(o, w)


def _branch_mix_kernel(y0_ref, y1_ref, y2_ref, g0_ref, g1_ref, g2_ref, w_ref, o_ref):
    mix = None
    for n, (y_ref, g_ref) in enumerate(((y0_ref, g0_ref), (y1_ref, g1_ref), (y2_ref, g2_ref))):
        proj = jnp.dot(y_ref[...].astype(BF16), w_ref[n], preferred_element_type=F32)
        term = jax.nn.sigmoid(g_ref[...]) * proj
        mix = term if mix is None else mix + term
    o_ref[...] = mix.astype(o_ref.dtype)


def branch_mix(ys, z, w_branch, *, tm, tn):
    m = z.shape[0]
    per = D_MODEL // tn
    yspec = pl.BlockSpec((tm, BRANCH_DIM), lambda i, j: (i, 0))

    def gspec(n):
        return pl.BlockSpec((tm, tn), lambda i, j, n=n: (i, Z_GATES // tn + n * per + j))

    return pl.pallas_call(
        _branch_mix_kernel,
        grid=(m // tm, per),
        in_specs=[yspec, yspec, yspec, gspec(0), gspec(1), gspec(2),
                  pl.BlockSpec((N_BRANCH, BRANCH_DIM, tn), lambda i, j: (0, 0, j))],
        out_specs=pl.BlockSpec((tm, tn), lambda i, j: (i, j)),
        out_shape=jax.ShapeDtypeStruct((m, D_MODEL), BF16),
        compiler_params=_cparams("parallel", "arbitrary"),
        name="branch_mix",
    )(*ys, z, z, z, w_branch)


def _add_norm_rows(o_ref, x_ref, f_ref, g_ref, rows, chunk=32):
    def body(c, carry):
        r = pl.ds(pl.multiple_of(c * chunk, chunk), chunk)
        o_ref[r, :] = x_ref[r, :] + _rms_rows(f_ref[r, :], g_ref[...])
        return carry
    lax.fori_loop(0, rows // chunk, body, 0)


def _out_proj_kernel(mix_ref, w_ref, x_ref, g_ref, o_ref, f_ref):
    f_ref[...] = jnp.dot(mix_ref[...], w_ref[...], preferred_element_type=F32)
    _add_norm_rows(o_ref, x_ref, f_ref, g_ref, x_ref.shape[0])


def out_proj_residual(mix, w_out, x, g, *, tm):
    m, d = x.shape
    return pl.pallas_call(
        _out_proj_kernel,
        grid=(m // tm,),
        in_specs=[pl.BlockSpec((tm, d), lambda i: (i, 0)),
                  pl.BlockSpec((d, d), lambda i: (0, 0)),
                  pl.BlockSpec((tm, d), lambda i: (i, 0)),
                  pl.BlockSpec((1, d), lambda i: (0, 0))],
        out_specs=pl.BlockSpec((tm, d), lambda i: (i, 0)),
        out_shape=jax.ShapeDtypeStruct((m, d), F32),
        scratch_shapes=[pltpu.VMEM((tm, d), F32)],
        compiler_params=_cparams("parallel"),
        name="out_proj",
    )(mix, w_out, x, g.reshape(1, d))


def _ffn_kernel(x_ref, g_in_ref, wg_ref, wu_ref, wd_ref, g_out_ref, o_ref, xn_ref, acc_ref):
    j = pl.program_id(1)

    @pl.when(j == 0)
    def _():
        _norm_rows_to(xn_ref, x_ref, g_in_ref, x_ref.shape[0])
        acc_ref[...] = jnp.zeros_like(acc_ref)

    xn = xn_ref[...]
    gate = jnp.dot(xn, wg_ref[...], preferred_element_type=F32)
    up = jnp.dot(xn, wu_ref[...], preferred_element_type=F32)
    act = (gate * jax.nn.sigmoid(gate) * up).astype(BF16)
    acc_ref[...] += jnp.dot(act, wd_ref[...], preferred_element_type=F32)

    @pl.when(j == pl.num_programs(1) - 1)
    def _():
        _add_norm_rows(o_ref, x_ref, acc_ref, g_out_ref, x_ref.shape[0])


def ffn(x, g_in, w_gu, w_down, g_out, *, tm, tf):
    m, d = x.shape
    nf = D_FF // tf
    return pl.pallas_call(
        _ffn_kernel,
        grid=(m // tm, nf),
        in_specs=[pl.BlockSpec((tm, d), lambda i, j: (i, 0)),
                  pl.BlockSpec((1, d), lambda i, j: (0, 0)),
                  pl.BlockSpec((d, tf), lambda i, j: (0, j)),
                  pl.BlockSpec((d, tf), lambda i, j: (0, j + nf)),
                  pl.BlockSpec((tf, d), lambda i, j: (j, 0)),
                  pl.BlockSpec((1, d), lambda i, j: (0, 0))],
        out_specs=pl.BlockSpec((tm, d), lambda i, j: (i, 0)),
        out_shape=jax.ShapeDtypeStruct((m, d), F32),
        scratch_shapes=[pltpu.VMEM((tm, d), BF16), pltpu.VMEM((tm, d), F32)],
        compiler_params=_cparams("parallel", "arbitrary"),
        name="ffn",
    )(x, g_in.reshape(1, d), w_gu, w_gu, w_down, g_out.reshape(1, d))


def _prep_w_in(w_in):
    splits = (CONV_DIM, CONV_DIM, CONV_DIM,
              GLA_HEADS * GLA_DK, GLA_HEADS * GLA_DK, GLA_HEADS * GLA_DV, GLA_LOWRANK, GLA_HEADS * GLA_DV,
              MLA_Q_RANK, MLA_KV_RANK, MLA_ROPE, N_BRANCH * D_MODEL)
    idx = np.cumsum(splits)[:-1].tolist()
    w_in = w_in.astype(BF16)
    cb, cc, ch, gq, gk, gv, ga, gg, mq, mkv, mkr, gates = jnp.split(w_in, idx, axis=1)
    d = w_in.shape[0]
    misc_pad = jnp.zeros((d, LANES - MLA_ROPE - GLA_LOWRANK), w_in.dtype)
    tail = jnp.zeros((d, NZ - Z_END), w_in.dtype)
    return jnp.concatenate([gates, cb, cc, ch, gq, gk, gv, gg, mq, mkv, mkr, ga, misc_pad, tail], axis=1)


def _prep_wuq(wuq):
    w = wuq.reshape(MLA_Q_RANK, MLA_HEADS, MLA_NOPE + MLA_ROPE)
    nope = w[:, :, :MLA_NOPE].reshape(MLA_Q_RANK, MLA_HEADS * MLA_NOPE)
    rope = w[:, :, MLA_NOPE:].reshape(MLA_Q_RANK, MLA_HEADS * MLA_ROPE)
    return jnp.concatenate([nope, rope], axis=1).astype(BF16)


def _layer_weights(l, norms, w_in, conv_w, gla_wa2, gla_ba, gla_onorm, mla_q_norm, mla_kv_norm, mla_wuq,
                   mla_wuk, mla_wuv, w_branch, w_out, ffn_w_gu, ffn_w_down):
    return dict(
        norms=norms[l], w_in=_prep_w_in(w_in[l]), conv_w=conv_w[l],
        wa2=jnp.pad(gla_wa2[l], ((MISC_GA, LANES - MISC_GA - GLA_LOWRANK), (0, 0))).astype(BF16),
        ba=gla_ba[l], onorm=gla_onorm[l],
        q_norm=mla_q_norm[l], kv_norm=mla_kv_norm[l], wuq=_prep_wuq(mla_wuq[l]),
        wuk_t=jnp.transpose(mla_wuk[l], (1, 2, 0)).astype(BF16),
        wuv_t=jnp.transpose(mla_wuv[l], (1, 0, 2)).astype(BF16),
        w_branch=w_branch[l].astype(BF16), w_out=w_out[l].astype(BF16),
        w_gu=ffn_w_gu[l].astype(BF16), w_down=ffn_w_down[l].astype(BF16))


def _finish_layer(x2, z, ys, p, *, tm):
    mix = branch_mix(ys, z, p["w_branch"], tm=tm, tn=512)
    x2 = out_proj_residual(mix, p["w_out"], x2, p["norms"][1], tm=min(tm, 256))
    return ffn(x2, p["norms"][2], p["w_gu"], p["w_down"], p["norms"][3], tm=tm, tf=512)


def _prompt_layer(x2, bsz, seq, rope_tab, p):
    z = norm_matmul(x2, p["norms"][0], p["w_in"], tm=1024, tn=NZ_TILE)
    z3 = z.reshape(bsz, seq, NZ)
    y_conv, conv_new = conv_prompt(z3, p["conv_w"], ts=256, tc=512)
    y_gla, gla_new = gla_prompt(z3, p["wa2"], p["ba"], p["onorm"], ts=256)
    q, kv32, kvb = mla_prep(z3, rope_tab[0], rope_tab[1], p["q_norm"], p["kv_norm"], p["wuq"], p["wuk_t"], tm=256)
    y_mla = mla_prompt_attend(q, kvb, p["wuv_t"], tq=128, tk=512, hb=2)
    ys = [y.reshape(bsz * seq, BRANCH_DIM) for y in (y_conv, y_gla, y_mla)]
    x2 = _finish_layer(x2, z, ys, p, tm=512)
    return x2, kv32, gla_new, conv_new


def _sample_layer(x2, bsz, n_tok, rope_tab, conv_buf, gla_s0, cache_t, page_table, layer, p):
    m = bsz * n_tok
    z = norm_matmul(x2, p["norms"][0], p["w_in"], tm=m, tn=NZ_TILE)
    y_conv, conv_new = conv_sample(z.reshape(bsz, n_tok * NZ), conv_buf.reshape(bsz, -1), p["conv_w"],
                                   n_tok=n_tok, tb=bsz)
    y_gla, gla_new = gla_sample(z, gla_s0, layer, p["wa2"], p["ba"], p["onorm"], n_tok=n_tok, tb=8)
    q, kv32, kvb = mla_prep(z.reshape(1, m, NZ), rope_tab[0], rope_tab[1], p["q_norm"], p["kv_norm"],
                            p["wuq"], p["wuk_t"], tm=256)
    qs = q.reshape(MLA_HEADS, bsz, n_tok, MLA_QK).transpose(1, 0, 2, 3).reshape(bsz, MLA_HEADS * n_tok, MLA_QK)
    kv_new = jnp.pad(kvb.reshape(bsz, n_tok, MLA_QK), ((0, 0), (0, PAGE_SIZE - n_tok), (0, 0)))
    o_lat = mla_sample_attend(qs, kv_new, cache_t, page_table, layer=layer, n_tok=n_tok, cp=32)
    o_h = o_lat.reshape(bsz, MLA_HEADS, n_tok, MLA_KV_RANK).transpose(1, 0, 2, 3).reshape(MLA_HEADS, m, MLA_KV_RANK)
    y_mla = head_matmul(o_h, p["wuv_t"])
    ys = [y_conv.reshape(m, BRANCH_DIM), y_gla, y_mla]
    x2 = _finish_layer(x2, z, ys, p, tm=m)
    return (x2, kv32.reshape(bsz, n_tok, MLA_QK), gla_new,
            conv_new.reshape(bsz, CONV_WIDTH - 1, CONV_DIM))


def kernel(x_prompt, x_sample, cache_mla, page_table, state_gla, state_conv, norms, w_in, conv_w, gla_wa2, gla_ba, gla_onorm, mla_q_norm, mla_kv_norm, mla_wuq, mla_wuk, mla_wuv, w_branch, w_out, ffn_w_gu, ffn_w_down):
    bp, sp, d = x_prompt.shape
    bs, ss, _ = x_sample.shape
    depth = norms.shape[0]
    past_len = page_table.shape[1] * PAGE_SIZE
    tab_p = _rope_tables(np.arange(sp), MLA_HEADS)
    tab_s = _rope_tables(past_len + np.arange(bs * ss) % ss, MLA_HEADS)
    xp = x_prompt.reshape(bp * sp, d)
    xs = x_sample.reshape(bs * ss, d)
    cache_t = jnp.swapaxes(cache_mla, 2, 3)
    outs = [[] for _ in range(6)]
    for l in range(depth):
        p = _layer_weights(l, norms, w_in, conv_w, gla_wa2, gla_ba, gla_onorm, mla_q_norm, mla_kv_norm, mla_wuq,
                           mla_wuk, mla_wuv, w_branch, w_out, ffn_w_gu, ffn_w_down)
        xp, kv_p, gla_p, conv_p = _prompt_layer(xp, bp, sp, tab_p, p)
        xs, kv_s, gla_s, conv_s = _sample_layer(xs, bs, ss, tab_s, state_conv[l], state_gla, cache_t,
                                                page_table, l, p)
        for acc, val in zip(outs, (kv_p, kv_s, gla_p, gla_s, conv_p, conv_s)):
            acc.append(val)
    return (xp.reshape(bp, sp, d), xs.reshape(bs, ss, d)) + tuple(jnp.stack(o) for o in outs)
```

```python
import functools
import math

import numpy as np
import jax
import jax.numpy as jnp
from jax import lax
from jax.experimental import pallas as pl
from jax.experimental.pallas import tpu as pltpu

F32 = jnp.float32
BF16 = jnp.bfloat16

D_MODEL = 2048
BRANCH_DIM = 1024
N_BRANCH = 3
CONV_DIM = BRANCH_DIM
CONV_WIDTH = 3
GLA_HEADS = 4
GLA_DK = 128
GLA_DV = BRANCH_DIM // GLA_HEADS
GLA_LOWRANK = 16
GLA_TAU = 16.0
GLA_CHUNK = 64
MLA_HEADS = 8
MLA_Q_RANK = 512
MLA_KV_RANK = 256
MLA_NOPE = 128
MLA_ROPE = 64
MLA_V = BRANCH_DIM // MLA_HEADS
MLA_SCALE = (MLA_NOPE + MLA_ROPE) ** -0.5
MLA_QK = MLA_KV_RANK + MLA_ROPE
ROPE_THETA = 10000.0
PAGE_SIZE = 128
D_FF = -(-8 * D_MODEL // (3 * 256)) * 256
RMS_EPS = 1e-6
NEG_INF = -1e30

LANES = 128
SUBLANES = 8
VMEM_LIMIT_BYTES = 56 * 1024 * 1024

Z_GATES = 0
Z_CONV = Z_GATES + N_BRANCH * D_MODEL
Z_GQ = Z_CONV + 3 * CONV_DIM
Z_GK = Z_GQ + GLA_HEADS * GLA_DK
Z_GV = Z_GK + GLA_HEADS * GLA_DK
Z_GG = Z_GV + GLA_HEADS * GLA_DV
Z_MQ = Z_GG + GLA_HEADS * GLA_DV
Z_MKV = Z_MQ + MLA_Q_RANK
Z_MISC = Z_MKV + MLA_KV_RANK
Z_END = Z_MISC + LANES
NZ_TILE = 512
NZ = -(-Z_END // NZ_TILE) * NZ_TILE
MISC_KR = 0
MISC_GA = MLA_ROPE


def _cparams(*sem):
    return pltpu.CompilerParams(dimension_semantics=sem, vmem_limit_bytes=VMEM_LIMIT_BYTES)


def _rms_rows(x, g):
    ms = jnp.mean(x * x, axis=-1, keepdims=True)
    return x * lax.rsqrt(ms + RMS_EPS) * g


def _norm_rows_to(dst_ref, src_ref, g_ref, rows, chunk=32):
    def body(c, carry):
        r0 = pl.multiple_of(c * chunk, chunk)
        dst_ref[pl.ds(r0, chunk), :] = _rms_rows(src_ref[pl.ds(r0, chunk), :], g_ref[...]).astype(dst_ref.dtype)
        return carry
    lax.fori_loop(0, rows // chunk, body, 0)


def _norm_mm_kernel(x_ref, g_ref, w_ref, o_ref, xn_ref):
    @pl.when(pl.program_id(1) == 0)
    def _():
        _norm_rows_to(xn_ref, x_ref, g_ref, x_ref.shape[0])
    o_ref[...] = _dot_nt(xn_ref[...], w_ref[...])


def norm_matmul(x, g, w_t, layer, *, tm, tn):
    m, k = x.shape
    n = w_t.shape[1]
    return pl.pallas_call(
        _norm_mm_kernel,
        grid=(m // tm, n // tn),
        in_specs=[pl.BlockSpec((tm, k), lambda i, j: (i, 0)),
                  pl.BlockSpec((1, k), lambda i, j: (0, 0)),
                  pl.BlockSpec((None, tn, k), lambda i, j: (layer, j, 0))],
        out_specs=pl.BlockSpec((tm, tn), lambda i, j: (i, j)),
        out_shape=jax.ShapeDtypeStruct((m, n), F32),
        scratch_shapes=[pltpu.VMEM((tm, k), BF16)],
        compiler_params=_cparams("parallel", "arbitrary"),
        name="in_proj",
    )(x, g.reshape(1, k), w_t)


def _conv_prompt_kernel(cb_ref, cc_ref, ch_ref, w_ref, y_ref, st_ref, carry_ref):
    i = pl.program_id(2)

    @pl.when(i == 0)
    def _():
        carry_ref[...] = jnp.zeros_like(carry_ref)

    u = cc_ref[...] * ch_ref[...]
    ts = u.shape[0]
    row = lax.broadcasted_iota(jnp.int32, u.shape, 0)
    prev1 = carry_ref[1:2, :]
    prev2 = carry_ref[0:1, :]
    um1 = jnp.where(row == 0, prev1, pltpu.roll(u, 1, axis=0))
    um2 = jnp.where(row == 0, prev2, jnp.where(row == 1, prev1, pltpu.roll(u, 2, axis=0)))
    w = w_ref[...]
    y = um2 * w[0:1, :] + um1 * w[1:2, :] + u * w[2:3, :]
    y_ref[...] = (cb_ref[...] * y).astype(y_ref.dtype)
    last = u[ts - 2:ts, :]
    carry_ref[...] = last

    @pl.when(i == pl.num_programs(2) - 1)
    def _():
        st_ref[...] = last


def conv_prompt(z3, conv_w, *, ts, tc):
    b, s, _ = z3.shape
    c = CONV_DIM
    base = Z_CONV // tc
    nc = c // tc

    def zspec(part):
        return pl.BlockSpec((None, ts, tc), lambda bi, ci, i, part=part: (bi, i, base + part * nc + ci))

    return pl.pallas_call(
        _conv_prompt_kernel,
        grid=(b, nc, s // ts),
        in_specs=[zspec(0), zspec(1), zspec(2),
                  pl.BlockSpec((CONV_WIDTH, tc), lambda bi, ci, i: (0, ci))],
        out_specs=[pl.BlockSpec((None, ts, tc), lambda bi, ci, i: (bi, i, ci)),
                   pl.BlockSpec((None, CONV_WIDTH - 1, tc), lambda bi, ci, i: (bi, 0, ci))],
        out_shape=[jax.ShapeDtypeStruct((b, s, c), BF16),
                   jax.ShapeDtypeStruct((b, CONV_WIDTH - 1, c), F32)],
        scratch_shapes=[pltpu.VMEM((CONV_WIDTH - 1, tc), F32)],
        compiler_params=_cparams("parallel", "parallel", "arbitrary"),
        name="conv_prompt",
    )(z3, z3, z3, conv_w)


def _conv_sample_kernel(cb_ref, cc_ref, ch_ref, b0_ref, b1_ref, w_ref, y_ref, u_ref, *, n_tok):
    u = cc_ref[...] * ch_ref[...]
    tpos = lax.broadcasted_iota(jnp.int32, u.shape, 0) % n_tok
    prev1 = b1_ref[...]
    um1 = jnp.where(tpos == 0, prev1, pltpu.roll(u, 1, axis=0))
    um2 = jnp.where(tpos == 0, b0_ref[...], jnp.where(tpos == 1, prev1, pltpu.roll(u, 2, axis=0)))
    w = w_ref[...]
    y = um2 * w[0:1, :] + um1 * w[1:2, :] + u * w[2:3, :]
    y_ref[...] = cb_ref[...] * y
    u_ref[...] = u


def conv_sample(z2, buf_rows, conv_w, *, n_tok, tc):
    m = z2.shape[0]
    c = CONV_DIM
    nc = c // tc
    base = Z_CONV // tc

    def zspec(part):
        return pl.BlockSpec((m, tc), lambda ci, part=part: (0, base + part * nc + ci))

    return pl.pallas_call(
        functools.partial(_conv_sample_kernel, n_tok=n_tok),
        grid=(nc,),
        in_specs=[zspec(0), zspec(1), zspec(2),
                  pl.BlockSpec((m, tc), lambda ci: (0, ci)),
                  pl.BlockSpec((m, tc), lambda ci: (0, nc + ci)),
                  pl.BlockSpec((CONV_WIDTH, tc), lambda ci: (0, ci))],
        out_specs=[pl.BlockSpec((m, tc), lambda ci: (0, ci)), pl.BlockSpec((m, tc), lambda ci: (0, ci))],
        out_shape=[jax.ShapeDtypeStruct((m, c), F32), jax.ShapeDtypeStruct((m, c), F32)],
        compiler_params=_cparams("parallel"),
        name="conv_sample",
    )(z2, z2, z2, buf_rows, buf_rows, conv_w)


def _log_sigmoid(x):
    return jnp.minimum(x, 0.0) - jnp.log1p(jnp.exp(-jnp.abs(x)))


def _split3(x):
    hi = x.astype(BF16)
    r1 = x - hi.astype(F32)
    mid = r1.astype(BF16)
    lo = (r1 - mid.astype(F32)).astype(BF16)
    return hi, mid, lo


def _col_bcast(row, width):
    n = row.shape[1]
    sq = jnp.transpose(jnp.broadcast_to(row, (n, n)))
    return jnp.concatenate([sq] * (width // n), axis=1)


def _dot_nt(a, b):
    return lax.dot_general(a, b, (((1,), (1,)), ((), ())), preferred_element_type=F32)


def _dot_tn(a, b):
    return lax.dot_general(a, b, (((0,), (0,)), ((), ())), preferred_element_type=F32)


def _gla_gate_norm(o, g, onorm):
    on = o * lax.rsqrt(jnp.mean(o * o, axis=-1, keepdims=True) + RMS_EPS) * onorm
    return on * (g * jax.nn.sigmoid(g))


def _pad_rows(x, rows):
    if x.shape[0] == rows:
        return x
    return jnp.concatenate([x, jnp.zeros((rows - x.shape[0], x.shape[1]), x.dtype)], axis=0)


def _gla_log_decay(misc, wa2_ref, ba_ref):
    lane = lax.broadcasted_iota(jnp.int32, misc.shape, 1)
    a_low = jnp.where((lane >= MISC_GA) & (lane < MISC_GA + GLA_LOWRANK), misc, 0.0).astype(BF16)
    a = jnp.dot(a_low, wa2_ref[...], preferred_element_type=F32) + ba_ref[...]
    return _log_sigmoid(a) * (1.0 / GLA_TAU)


def _gla_prompt_kernel(q_ref, k_ref, v_ref, g_ref, a_ref, wa2_ref, ba_ref, on_ref, y_ref, st_ref, s_ref, *, ts):
    i = pl.program_id(1)

    @pl.when(i == 0)
    def _():
        s_ref[...] = jnp.zeros_like(s_ref)

    c = GLA_CHUNK
    pc = LANES
    rr = lax.broadcasted_iota(jnp.int32, (c, pc), 0)
    cc = lax.broadcasted_iota(jnp.int32, (c, pc), 1)
    tril = rr >= cc
    tril_b = tril.astype(BF16)
    scale = GLA_DK ** -0.5
    for ci in range(ts // c):
        r = slice(ci * c, (ci + 1) * c)
        la = _gla_log_decay(a_ref[r, :], wa2_ref, ba_ref)
        big_l = None
        for term in _split3(_pad_rows(la, pc)):
            part = jnp.dot(tril_b, term, preferred_element_type=F32)
            big_l = part if big_l is None else big_l + part
        l_last = big_l[c - 1:c, :]
        e_pos = jnp.exp(big_l)
        e_neg = jnp.exp(-big_l)
        e_dec = jnp.exp(l_last - big_l)
        e_last = jnp.exp(l_last)
        for h in range(GLA_HEADS):
            hs = slice(h * GLA_DK, (h + 1) * GLA_DK)
            vs = slice(h * GLA_DV, (h + 1) * GLA_DV)
            k = k_ref[r, hs]
            q_t = (q_ref[r, hs] * scale * e_pos[:, hs]).astype(BF16)
            k_t = _pad_rows(k * e_neg[:, hs], pc).astype(BF16)
            k_dec = _pad_rows(k * e_dec[:, hs], pc).astype(BF16)
            v = _pad_rows(v_ref[r, vs], pc).astype(BF16)
            attn = jnp.where(tril, _dot_nt(q_t, k_t), 0.0).astype(BF16)
            s_old = s_ref[h]
            o = (jnp.dot(q_t, s_old.astype(BF16), preferred_element_type=F32)
                 + jnp.dot(attn, v, preferred_element_type=F32))
            s_ref[h] = s_old * _col_bcast(e_last[:, hs], GLA_DV) + _dot_tn(k_dec, v)
            y_ref[r, vs] = _gla_gate_norm(o, g_ref[r, vs], on_ref[...]).astype(y_ref.dtype)

    @pl.when(i == pl.num_programs(1) - 1)
    def _():
        st_ref[...] = s_ref[...]


def gla_prompt(z3, wa2, ba, onorm, *, ts):
    b, s, _ = z3.shape
    hk = GLA_HEADS * GLA_DK
    hv = GLA_HEADS * GLA_DV
    return pl.pallas_call(
        functools.partial(_gla_prompt_kernel, ts=ts),
        grid=(b, s // ts),
        in_specs=[pl.BlockSpec((None, ts, hk), lambda bi, i: (bi, i, Z_GQ // hk)),
                  pl.BlockSpec((None, ts, hk), lambda bi, i: (bi, i, Z_GK // hk)),
                  pl.BlockSpec((None, ts, hv), lambda bi, i: (bi, i, Z_GV // hv)),
                  pl.BlockSpec((None, ts, hv), lambda bi, i: (bi, i, Z_GG // hv)),
                  pl.BlockSpec((None, ts, LANES), lambda bi, i: (bi, i, Z_MISC // LANES)),
                  pl.BlockSpec((LANES, hk), lambda bi, i: (0, 0)),
                  pl.BlockSpec((1, hk), lambda bi, i: (0, 0)),
                  pl.BlockSpec((1, GLA_DV), lambda bi, i: (0, 0))],
        out_specs=[pl.BlockSpec((None, ts, hv), lambda bi, i: (bi, i, 0)),
                   pl.BlockSpec((None, GLA_HEADS, GLA_DK, GLA_DV), lambda bi, i: (bi, 0, 0, 0))],
        out_shape=[jax.ShapeDtypeStruct((b, s, hv), BF16),
                   jax.ShapeDtypeStruct((b, GLA_HEADS, GLA_DK, GLA_DV), F32)],
        scratch_shapes=[pltpu.VMEM((GLA_HEADS, GLA_DK, GLA_DV), F32)],
        compiler_params=_cparams("parallel", "arbitrary"),
        name="gla_prompt",
    )(z3, z3, z3, z3, z3, wa2, ba.reshape(1, hk), onorm.reshape(1, GLA_DV))


def _gla_sample_kernel(q_ref, k_ref, v_ref, g_ref, a_ref, wa2_ref, ba_ref, on_ref, s0_ref, y_ref, st_ref,
                       *, n_tok, n_groups):
    rows = SUBLANES
    per = rows // n_tok
    pr = 2 * SUBLANES
    pc = LANES
    rr = lax.broadcasted_iota(jnp.int32, (pr, pc), 0)
    cc = lax.broadcasted_iota(jnp.int32, (pr, pc), 1)
    causal = (rr // n_tok == cc // n_tok) & (rr >= cc)
    trow = lax.broadcasted_iota(jnp.int32, (rows, 1), 0)
    tpos = trow % n_tok
    scale = GLA_DK ** -0.5

    def body(p, carry):
        r0 = pl.multiple_of(p * rows, rows)
        r = pl.ds(r0, rows)
        la = _gla_log_decay(a_ref[r, :], wa2_ref, ba_ref)
        big_l = la
        for d in range(1, n_tok):
            big_l = big_l + jnp.where(tpos >= d, pltpu.roll(la, d, axis=0), 0.0)
        l_last = jnp.zeros_like(big_l)
        for j in range(per):
            last_j = big_l[(j + 1) * n_tok - 1:(j + 1) * n_tok, :]
            l_last = jnp.where(trow // n_tok == j, last_j, l_last)
        e_pos = jnp.exp(big_l)
        e_neg = jnp.exp(-big_l)
        e_dec = jnp.exp(l_last - big_l)
        for h in range(GLA_HEADS):
            hs = slice(h * GLA_DK, (h + 1) * GLA_DK)
            vs = slice(h * GLA_DV, (h + 1) * GLA_DV)
            k = k_ref[r, hs]
            q_t = _pad_rows(q_ref[r, hs] * scale * e_pos[:, hs], pr).astype(BF16)
            k_t = _pad_rows(k * e_neg[:, hs], pc).astype(BF16)
            k_dec = k * e_dec[:, hs]
            v = _pad_rows(v_ref[r, vs], pc).astype(BF16)
            attn = jnp.where(causal, _dot_nt(q_t, k_t), 0.0).astype(BF16)
            o = jnp.dot(attn, v, preferred_element_type=F32)[0:rows]
            for j in range(per):
                seq = p * per + j
                in_seq = trow // n_tok == j
                s_old = s0_ref[seq, h]
                o_inter = jnp.dot(q_t, s_old.astype(BF16), preferred_element_type=F32)[0:rows]
                o = o + jnp.where(in_seq, o_inter, 0.0)
                e_last = jnp.exp(big_l[(j + 1) * n_tok - 1:(j + 1) * n_tok, hs])
                k_dec_j = _pad_rows(jnp.where(in_seq, k_dec, 0.0), pc).astype(BF16)
                st_ref[seq, h] = s_old * _col_bcast(e_last, GLA_DV) + _dot_tn(k_dec_j, v)
            y_ref[r, vs] = _gla_gate_norm(o, g_ref[r, vs], on_ref[...])
        return carry

    lax.fori_loop(0, n_groups, body, 0)


def gla_sample(z2, s0_all, layer, wa2, ba, onorm, *, n_tok, tb):
    m = z2.shape[0]
    b = m // n_tok
    hk = GLA_HEADS * GLA_DK
    hv = GLA_HEADS * GLA_DV
    tr = tb * n_tok
    st_spec = pl.BlockSpec((tb, GLA_HEADS, GLA_DK, GLA_DV), lambda i: (i, 0, 0, 0))
    s0_spec = pl.BlockSpec((None, tb, GLA_HEADS, GLA_DK, GLA_DV), lambda i: (layer, i, 0, 0, 0))
    return pl.pallas_call(
        functools.partial(_gla_sample_kernel, n_tok=n_tok, n_groups=tr // SUBLANES),
        grid=(b // tb,),
        in_specs=[pl.BlockSpec((tr, hk), lambda i: (i, Z_GQ // hk)),
                  pl.BlockSpec((tr, hk), lambda i: (i, Z_GK // hk)),
                  pl.BlockSpec((tr, hv), lambda i: (i, Z_GV // hv)),
                  pl.BlockSpec((tr, hv), lambda i: (i, Z_GG // hv)),
                  pl.BlockSpec((tr, LANES), lambda i: (i, Z_MISC // LANES)),
                  pl.BlockSpec((LANES, hk), lambda i: (0, 0)),
                  pl.BlockSpec((1, hk), lambda i: (0, 0)),
                  pl.BlockSpec((1, GLA_DV), lambda i: (0, 0)),
                  s0_spec],
        out_specs=[pl.BlockSpec((tr, hv), lambda i: (i, 0)), st_spec],
        out_shape=[jax.ShapeDtypeStruct((m, hv), F32),
                   jax.ShapeDtypeStruct(s0_all.shape[1:], F32)],
        compiler_params=_cparams("parallel"),
        name="gla_sample",
    )(z2, z2, z2, z2, z2, wa2, ba.reshape(1, hk), onorm.reshape(1, GLA_DV), s0_all)


def _rope_tables(pos, reps):
    half = MLA_ROPE // 2
    inv_freq = ROPE_THETA ** (-np.arange(half, dtype=np.float64) / half)
    ang = np.asarray(pos, np.float64)[:, None] * inv_freq[None, :]
    cos = np.concatenate([np.cos(ang), np.cos(ang)], axis=1)
    sin = np.concatenate([-np.sin(ang), np.sin(ang)], axis=1)
    return (jnp.asarray(np.tile(cos, (1, reps)), F32), jnp.asarray(np.tile(sin, (1, reps)), F32))


def _rope(x, cos, sin):
    n = x.shape[1]
    half = MLA_ROPE // 2
    lane = lax.broadcasted_iota(jnp.int32, x.shape, 1)
    partner = jnp.where(lane % MLA_ROPE < half, pltpu.roll(x, n - half, axis=1), pltpu.roll(x, half, axis=1))
    return x * cos + partner * sin


def _mla_prep_kernel(mq_ref, mkv_ref, misc_ref, cos_ref, sin_ref, qn_ref, kvn_ref, wuq_ref, wuk_ref,
                     q_out, kv32_out, kvb_out, cq_ref, q_ref):
    tm = mq_ref.shape[0]
    _norm_rows_to(cq_ref, mq_ref, qn_ref, tm)
    q_ref[...] = jnp.dot(cq_ref[...], wuq_ref[...], preferred_element_type=F32)
    n_nope = MLA_HEADS * MLA_NOPE
    q_rope = _rope(q_ref[:, n_nope:], cos_ref[...], sin_ref[...])
    for h in range(MLA_HEADS):
        q_nope = q_ref[:, h * MLA_NOPE:(h + 1) * MLA_NOPE].astype(BF16)
        q_out[h, :, 0:MLA_KV_RANK] = jnp.dot(q_nope, wuk_ref[h], preferred_element_type=F32).astype(BF16)
        q_out[h, :, MLA_KV_RANK:MLA_QK] = q_rope[:, h * MLA_ROPE:(h + 1) * MLA_ROPE].astype(BF16)
    c = _rms_rows(mkv_ref[...], kvn_ref[...])
    kr = _rope(misc_ref[...], cos_ref[:, 0:LANES], sin_ref[:, 0:LANES])[:, MISC_KR:MISC_KR + MLA_ROPE]
    kv32_out[:, 0:MLA_KV_RANK] = c
    kv32_out[:, MLA_KV_RANK:MLA_QK] = kr
    kvb_out[:, 0:MLA_KV_RANK] = c.astype(BF16)
    kvb_out[:, MLA_KV_RANK:MLA_QK] = kr.astype(BF16)


def mla_prep(z3, cos, sin, q_norm, kv_norm, wuq, wuk_t, *, tm):
    b, s, _ = z3.shape
    nq = MLA_HEADS * (MLA_NOPE + MLA_ROPE)
    nr = MLA_HEADS * MLA_ROPE
    const2 = lambda bi, i: (0, 0)
    return pl.pallas_call(
        _mla_prep_kernel,
        grid=(b, s // tm),
        in_specs=[pl.BlockSpec((None, tm, MLA_Q_RANK), lambda bi, i: (bi, i, Z_MQ // MLA_Q_RANK)),
                  pl.BlockSpec((None, tm, MLA_KV_RANK), lambda bi, i: (bi, i, Z_MKV // MLA_KV_RANK)),
                  pl.BlockSpec((None, tm, LANES), lambda bi, i: (bi, i, Z_MISC // LANES)),
                  pl.BlockSpec((tm, nr), lambda bi, i: (i, 0)),
                  pl.BlockSpec((tm, nr), lambda bi, i: (i, 0)),
                  pl.BlockSpec((1, MLA_Q_RANK), const2),
                  pl.BlockSpec((1, MLA_KV_RANK), const2),
                  pl.BlockSpec((MLA_Q_RANK, nq), const2),
                  pl.BlockSpec((MLA_HEADS, MLA_NOPE, MLA_KV_RANK), lambda bi, i: (0, 0, 0))],
        out_specs=[pl.BlockSpec((None, MLA_HEADS, tm, MLA_QK), lambda bi, i: (bi, 0, i, 0)),
                   pl.BlockSpec((None, tm, MLA_QK), lambda bi, i: (bi, i, 0)),
                   pl.BlockSpec((None, tm, MLA_QK), lambda bi, i: (bi, i, 0))],
        out_shape=[jax.ShapeDtypeStruct((b, MLA_HEADS, s, MLA_QK), BF16),
                   jax.ShapeDtypeStruct((b, s, MLA_QK), F32),
                   jax.ShapeDtypeStruct((b, s, MLA_QK), BF16)],
        scratch_shapes=[pltpu.VMEM((tm, MLA_Q_RANK), BF16), pltpu.VMEM((tm, nq), F32)],
        compiler_params=_cparams("parallel", "parallel"),
        name="mla_prep",
    )(z3, z3, z3, cos, sin, q_norm.reshape(1, -1), kv_norm.reshape(1, -1), wuq, wuk_t)


SOFTMAX_EXP2_SCALE = MLA_SCALE * math.log2(math.e)


def _online_softmax(s, kv_c_t, m_prev, l_prev, acc_prev, contract_kv_lanes):
    m_new = jnp.maximum(m_prev, jnp.max(s, axis=-1, keepdims=True))
    corr = jnp.exp2((m_prev - m_new) * SOFTMAX_EXP2_SCALE)
    p = jnp.exp2((s - m_new) * SOFTMAX_EXP2_SCALE)
    pv = _dot_nt(p.astype(BF16), kv_c_t) if contract_kv_lanes else jnp.dot(
        p.astype(BF16), kv_c_t, preferred_element_type=F32)
    return m_new, l_prev * corr + jnp.sum(p, axis=-1, keepdims=True), acc_prev * corr + pv


def _mla_prompt_kernel(q_ref, kv_ref, wuv_ref, y_ref, m_ref, l_ref, acc_ref, *, tq, tk, hb):
    i = pl.program_id(1)
    rb = hb * tq
    m_ref[...] = jnp.full_like(m_ref, NEG_INF)
    l_ref[...] = jnp.zeros_like(l_ref)
    acc_ref[...] = jnp.zeros_like(acc_ref)

    def chunk(j, masked):
        kv = kv_ref[pl.ds(pl.multiple_of(j * tk, tk), tk), :]
        kv_c = kv[:, 0:MLA_KV_RANK]
        n_blocks = MLA_HEADS // hb

        def scores(r):
            s = _dot_nt(q_ref[r * hb:(r + 1) * hb].reshape(rb, MLA_QK), kv)
            if masked:
                qpos = i * tq + lax.broadcasted_iota(jnp.int32, s.shape, 0) % tq
                kpos = j * tk + lax.broadcasted_iota(jnp.int32, s.shape, 1)
                s = jnp.where(kpos <= qpos, s, NEG_INF)
            return s

        s_next = scores(0)
        for r in range(n_blocks):
            s = s_next
            if r + 1 < n_blocks:
                s_next = scores(r + 1)
            rs = slice(r * rb, (r + 1) * rb)
            m_ref[rs], l_ref[rs], acc_ref[rs] = _online_softmax(s, kv_c, m_ref[rs], l_ref[rs], acc_ref[rs], False)

    def full_chunk(j, carry):
        chunk(j, False)
        return carry

    last = (i * tq + tq - 1) // tk
    lax.fori_loop(0, last, full_chunk, 0)
    chunk(last, True)

    for h in range(MLA_HEADS):
        hs = slice(h * tq, (h + 1) * tq)
        o = (acc_ref[hs] / l_ref[hs]).astype(BF16)
        y_ref[:, h * MLA_V:(h + 1) * MLA_V] = jnp.dot(o, wuv_ref[h], preferred_element_type=F32).astype(y_ref.dtype)


def mla_prompt_attend(q, kvb, wuv_t, *, tq, tk, hb):
    b, _, s, _ = q.shape
    rows = MLA_HEADS * tq
    return pl.pallas_call(
        functools.partial(_mla_prompt_kernel, tq=tq, tk=tk, hb=hb),
        grid=(b, s // tq),
        in_specs=[pl.BlockSpec((None, MLA_HEADS, tq, MLA_QK), lambda bi, i: (bi, 0, i, 0)),
                  pl.BlockSpec((None, s, MLA_QK), lambda bi, i: (bi, 0, 0)),
                  pl.BlockSpec((MLA_HEADS, MLA_KV_RANK, MLA_V), lambda bi, i: (0, 0, 0))],
        out_specs=pl.BlockSpec((None, tq, BRANCH_DIM), lambda bi, i: (bi, i, 0)),
        out_shape=jax.ShapeDtypeStruct((b, s, BRANCH_DIM), BF16),
        scratch_shapes=[pltpu.VMEM((rows, 1), F32), pltpu.VMEM((rows, 1), F32),
                        pltpu.VMEM((rows, MLA_KV_RANK), F32)],
        compiler_params=_cparams("parallel", "arbitrary"),
        name="mla_prompt_attn",
    )(q, kvb, wuv_t)


def _lane_row(col):
    rows = col.shape[0]
    sq = jnp.transpose(jnp.broadcast_to(_pad_rows(col, LANES), (LANES, LANES)))
    return sq[0:1, 0:rows]


def _absorb_keys(s, kt_c, carry):
    m_prev, l_prev, acc_t = carry
    m_new = jnp.maximum(m_prev, jnp.max(s, axis=-1, keepdims=True))
    corr = jnp.exp2((m_prev - m_new) * SOFTMAX_EXP2_SCALE)
    p = jnp.exp2((s - m_new) * SOFTMAX_EXP2_SCALE)
    l_new = l_prev * corr + jnp.sum(p, axis=-1, keepdims=True)
    return m_new, l_new, acc_t * _lane_row(corr) + _dot_nt(kt_c, p.astype(BF16))


def _mla_sample_kernel(pt_ref, q_ref, new_ref, cache_ref, o_ref, f0, f1, h0, h1, s0, s1, sem,
                       *, layer, n_tok, n_chunks, cp):
    b = pl.program_id(0)
    nb = pl.num_programs(0)
    fbuf, hbuf, sbuf = (f0, f1), (h0, h1), (s0, s1)

    def page_copy(bi, c, p, slot):
        page = pt_ref[bi, c * cp + p]
        return pltpu.make_async_copy(cache_ref.at[layer, page], fbuf[slot].at[p], sem.at[slot])

    def start_chunk(bi, c, slot):
        for p in range(cp):
            page_copy(bi, c, p, slot).start()

    def start_ahead(c, slot):
        @pl.when(c < n_chunks)
        def _():
            start_chunk(b, c, slot)

        @pl.when((c >= n_chunks) & (b + 1 < nb))
        def _():
            start_chunk(b + 1, c - n_chunks, slot)

    def score_chunk(c, slot):
        for p in range(cp):
            page_copy(b, c, p, slot).wait()
        for p in range(cp):
            hbuf[slot][:, p * PAGE_SIZE:(p + 1) * PAGE_SIZE] = fbuf[slot][p].astype(BF16)
        sbuf[slot][...] = jnp.dot(q, hbuf[slot][...], preferred_element_type=F32)

    def absorb_chunk(slot, carry):
        return _absorb_keys(sbuf[slot][...], hbuf[slot][0:MLA_KV_RANK, :], carry)

    @pl.when(b == 0)
    def _():
        start_chunk(b, 0, 0)
        start_chunk(b, 1, 1)

    q = q_ref[...]
    rows = q.shape[0]
    score_chunk(0, 0)

    def chunk_pair(c2, carry):
        c = 2 * c2
        start_ahead(c + 2, 0)
        score_chunk(c + 1, 1)
        carry = absorb_chunk(0, carry)
        start_ahead(c + 3, 1)
        score_chunk(c + 2, 0)
        return absorb_chunk(1, carry)

    carry = (jnp.full((rows, 1), NEG_INF, F32), jnp.zeros((rows, 1), F32), jnp.zeros((MLA_KV_RANK, rows), F32))
    carry = lax.fori_loop(0, n_chunks // 2 - 1, chunk_pair, carry)
    start_ahead(n_chunks, 0)
    score_chunk(n_chunks - 1, 1)
    carry = absorb_chunk(0, carry)
    start_ahead(n_chunks + 1, 1)
    carry = absorb_chunk(1, carry)

    kt_new = new_ref[...]
    s = jnp.dot(q, kt_new, preferred_element_type=F32)
    tpos = lax.broadcasted_iota(jnp.int32, s.shape, 0) % n_tok
    kpos = lax.broadcasted_iota(jnp.int32, s.shape, 1)
    m, l, acc_t = _absorb_keys(jnp.where(kpos <= tpos, s, NEG_INF), kt_new[0:MLA_KV_RANK, :], carry)
    o_t = acc_t / _lane_row(l)
    o_sq = jnp.transpose(jnp.concatenate([o_t, jnp.zeros((MLA_KV_RANK, LANES - rows), F32)], axis=1))
    o_ref[...] = o_sq[0:rows, :].astype(o_ref.dtype)


def mla_sample_attend(q, kt_new_pad, cache_t, page_table, *, layer, n_tok, cp):
    b, rows, _ = q.shape
    n_pages = page_table.shape[1]
    n_chunks = n_pages // cp
    keys = cp * PAGE_SIZE
    assert n_chunks % 2 == 0 and n_chunks * cp == n_pages and rows <= LANES
    grid_spec = pltpu.PrefetchScalarGridSpec(
        num_scalar_prefetch=1,
        grid=(b,),
        in_specs=[pl.BlockSpec((None, rows, MLA_QK), lambda bi, pt: (bi, 0, 0)),
                  pl.BlockSpec((None, MLA_QK, PAGE_SIZE), lambda bi, pt: (bi, 0, 0)),
                  pl.BlockSpec(memory_space=pl.ANY)],
        out_specs=pl.BlockSpec((None, rows, MLA_KV_RANK), lambda bi, pt: (bi, 0, 0)),
        scratch_shapes=[pltpu.VMEM((cp, MLA_QK, PAGE_SIZE), F32), pltpu.VMEM((cp, MLA_QK, PAGE_SIZE), F32),
                        pltpu.VMEM((MLA_QK, keys), BF16), pltpu.VMEM((MLA_QK, keys), BF16),
                        pltpu.VMEM((rows, keys), F32), pltpu.VMEM((rows, keys), F32),
                        pltpu.SemaphoreType.DMA((2,))],
    )
    return pl.pallas_call(
        functools.partial(_mla_sample_kernel, layer=layer, n_tok=n_tok, n_chunks=n_chunks, cp=cp),
        grid_spec=grid_spec,
        out_shape=jax.ShapeDtypeStruct((b, rows, MLA_KV_RANK), BF16),
        compiler_params=_cparams("arbitrary"),
        name="mla_sample_attn",
    )(page_table, q, kt_new_pad, cache_t)


def _head_mm_kernel(o_ref, w_ref, y_ref):
    y_ref[...] = jnp.dot(o_ref[...], w_ref[...], preferred_element_type=F32).astype(y_ref.dtype)


def head_matmul(o, w):
    h, m, k = o.shape
    n = w.shape[2]
    return pl.pallas_call(
        _head_mm_kernel,
        grid=(h,),
        in_specs=[pl.BlockSpec((None, m, k), lambda i: (i, 0, 0)),
                  pl.BlockSpec((None, k, n), lambda i: (i, 0, 0))],
        out_specs=pl.BlockSpec((m, n), lambda i: (0, i)),
        out_shape=jax.ShapeDtypeStruct((m, h * n), BF16),
        compiler_params=_cparams("parallel"),
        name="mla_value_up",
    )(o, w)


def _branch_mix_kernel(y0_ref, y1_ref, y2_ref, g0_ref, g1_ref, g2_ref, w_ref, o_ref):
    mix = None
    for n, (y_ref, g_ref) in enumerate(((y0_ref, g0_ref), (y1_ref, g1_ref), (y2_ref, g2_ref))):
        proj = jnp.dot(y_ref[...].astype(BF16), w_ref[n], preferred_element_type=F32)
        term = jax.nn.sigmoid(g_ref[...]) * proj
        mix = term if mix is None else mix + term
    o_ref[...] = mix.astype(o_ref.dtype)


def branch_mix(ys, z, w_branch, layer, *, tm, tn):
    m = z.shape[0]
    per = D_MODEL // tn
    yspec = pl.BlockSpec((tm, BRANCH_DIM), lambda i, j: (i, 0))

    def gspec(n):
        return pl.BlockSpec((tm, tn), lambda i, j, n=n: (i, Z_GATES // tn + n * per + j))

    return pl.pallas_call(
        _branch_mix_kernel,
        grid=(m // tm, per),
        in_specs=[yspec, yspec, yspec, gspec(0), gspec(1), gspec(2),
                  pl.BlockSpec((None, N_BRANCH, BRANCH_DIM, tn), lambda i, j: (layer, 0, 0, j))],
        out_specs=pl.BlockSpec((tm, tn), lambda i, j: (i, j)),
        out_shape=jax.ShapeDtypeStruct((m, D_MODEL), BF16),
        compiler_params=_cparams("parallel", "arbitrary"),
        name="branch_mix",
    )(*ys, z, z, z, w_branch)


def _add_norm_rows(o_ref, x_ref, f_ref, g_ref, rows, chunk=32):
    def body(c, carry):
        r = pl.ds(pl.multiple_of(c * chunk, chunk), chunk)
        o_ref[r, :] = x_ref[r, :] + _rms_rows(f_ref[r, :], g_ref[...])
        return carry
    lax.fori_loop(0, rows // chunk, body, 0)


def _out_proj_kernel(mix_ref, w_ref, x_ref, g_ref, o_ref, f_ref):
    f_ref[...] = jnp.dot(mix_ref[...], w_ref[...], preferred_element_type=F32)
    _add_norm_rows(o_ref, x_ref, f_ref, g_ref, x_ref.shape[0])


def out_proj_residual(mix, w_out, layer, x, g, *, tm):
    m, d = x.shape
    return pl.pallas_call(
        _out_proj_kernel,
        grid=(m // tm,),
        in_specs=[pl.BlockSpec((tm, d), lambda i: (i, 0)),
                  pl.BlockSpec((None, d, d), lambda i: (layer, 0, 0)),
                  pl.BlockSpec((tm, d), lambda i: (i, 0)),
                  pl.BlockSpec((1, d), lambda i: (0, 0))],
        out_specs=pl.BlockSpec((tm, d), lambda i: (i, 0)),
        out_shape=jax.ShapeDtypeStruct((m, d), F32),
        scratch_shapes=[pltpu.VMEM((tm, d), F32)],
        compiler_params=_cparams("parallel"),
        name="out_proj",
    )(mix, w_out, x, g.reshape(1, d))


def _ffn_kernel(x_ref, g_in_ref, wg_ref, wu_ref, wd_ref, g_out_ref, o_ref, xn_ref, acc_ref):
    j = pl.program_id(1)

    @pl.when(j == 0)
    def _():
        _norm_rows_to(xn_ref, x_ref, g_in_ref, x_ref.shape[0])
        acc_ref[...] = jnp.zeros_like(acc_ref)

    xn = xn_ref[...]
    gate = jnp.dot(xn, wg_ref[...], preferred_element_type=F32)
    up = jnp.dot(xn, wu_ref[...], preferred_element_type=F32)
    act = (gate * jax.nn.sigmoid(gate) * up).astype(BF16)
    acc_ref[...] += jnp.dot(act, wd_ref[...], preferred_element_type=F32)

    @pl.when(j == pl.num_programs(1) - 1)
    def _():
        _add_norm_rows(o_ref, x_ref, acc_ref, g_out_ref, x_ref.shape[0])


def ffn(x, g_in, w_gu, w_down, layer, g_out, *, tm, tf):
    m, d = x.shape
    nf = D_FF // tf
    return pl.pallas_call(
        _ffn_kernel,
        grid=(m // tm, nf),
        in_specs=[pl.BlockSpec((tm, d), lambda i, j: (i, 0)),
                  pl.BlockSpec((1, d), lambda i, j: (0, 0)),
                  pl.BlockSpec((None, d, tf), lambda i, j: (layer, 0, j)),
                  pl.BlockSpec((None, d, tf), lambda i, j: (layer, 0, j + nf)),
                  pl.BlockSpec((None, tf, d), lambda i, j: (layer, j, 0)),
                  pl.BlockSpec((1, d), lambda i, j: (0, 0))],
        out_specs=pl.BlockSpec((tm, d), lambda i, j: (i, 0)),
        out_shape=jax.ShapeDtypeStruct((m, d), F32),
        scratch_shapes=[pltpu.VMEM((tm, d), BF16), pltpu.VMEM((tm, d), F32)],
        compiler_params=_cparams("parallel", "arbitrary"),
        name="ffn",
    )(x, g_in.reshape(1, d), w_gu, w_gu, w_down, g_out.reshape(1, d))


def _prep_w_in_t(w_in):
    splits = (CONV_DIM, CONV_DIM, CONV_DIM,
              GLA_HEADS * GLA_DK, GLA_HEADS * GLA_DK, GLA_HEADS * GLA_DV, GLA_LOWRANK, GLA_HEADS * GLA_DV,
              MLA_Q_RANK, MLA_KV_RANK, MLA_ROPE, N_BRANCH * D_MODEL)
    idx = np.cumsum(splits)[:-1].tolist()
    w_t = jnp.swapaxes(w_in, 1, 2)
    cb, cc, ch, gq, gk, gv, ga, gg, mq, mkv, mkr, gates = jnp.split(w_t, idx, axis=1)
    depth, _, d = w_t.shape
    zeros = jnp.zeros((depth, LANES - MLA_ROPE - GLA_LOWRANK + NZ - Z_END, d), w_t.dtype)
    return jnp.concatenate([gates, cb, cc, ch, gq, gk, gv, gg, mq, mkv, mkr, ga, zeros], axis=1).astype(BF16)


def _prep_wuq(wuq):
    w = wuq.reshape(MLA_Q_RANK, MLA_HEADS, MLA_NOPE + MLA_ROPE)
    nope = w[:, :, :MLA_NOPE].reshape(MLA_Q_RANK, MLA_HEADS * MLA_NOPE)
    rope = w[:, :, MLA_NOPE:].reshape(MLA_Q_RANK, MLA_HEADS * MLA_ROPE)
    return jnp.concatenate([nope, rope], axis=1).astype(BF16)


def _stacked_weights(w_in, w_branch, w_out, ffn_w_gu, ffn_w_down):
    return dict(w_in_t=_prep_w_in_t(w_in), w_branch=w_branch.astype(BF16), w_out=w_out.astype(BF16),
                w_gu=ffn_w_gu.astype(BF16), w_down=ffn_w_down.astype(BF16))


def _layer_weights(l, big, norms, conv_w, gla_wa2, gla_ba, gla_onorm, mla_q_norm, mla_kv_norm, mla_wuq,
                   mla_wuk, mla_wuv):
    return dict(
        big, layer=l, norms=norms[l], conv_w=conv_w[l],
        wa2=jnp.pad(gla_wa2[l], ((MISC_GA, LANES - MISC_GA - GLA_LOWRANK), (0, 0))).astype(BF16),
        ba=gla_ba[l], onorm=gla_onorm[l],
        q_norm=mla_q_norm[l], kv_norm=mla_kv_norm[l], wuq=_prep_wuq(mla_wuq[l]),
        wuk_t=jnp.transpose(mla_wuk[l], (1, 2, 0)).astype(BF16),
        wuv_t=jnp.transpose(mla_wuv[l], (1, 0, 2)).astype(BF16))


PROMPT_TILES = dict(in_proj_tm=1024, conv_ts=256, conv_tc=512, gla_ts=256, prep_tm=256,
                    attn_tq=128, attn_tk=512, attn_heads_per_block=2, row_tm=512)
SAMPLE_TILES = dict(conv_tc=256, gla_tb=8, prep_tm=256, attn_pages_per_chunk=32)
COL_TILE = 512
OUT_PROJ_TM = 256


def _finish_layer(x2, z, ys, p, *, tm):
    l = p["layer"]
    mix = branch_mix(ys, z, p["w_branch"], l, tm=tm, tn=COL_TILE)
    x2 = out_proj_residual(mix, p["w_out"], l, x2, p["norms"][1], tm=OUT_PROJ_TM)
    return ffn(x2, p["norms"][2], p["w_gu"], p["w_down"], l, p["norms"][3], tm=tm, tf=COL_TILE)


def _prompt_layer(x2, bsz, seq, rope_tab, p):
    t = PROMPT_TILES
    z = norm_matmul(x2, p["norms"][0], p["w_in_t"], p["layer"], tm=t["in_proj_tm"], tn=NZ_TILE)
    z3 = z.reshape(bsz, seq, NZ)
    y_conv, conv_new = conv_prompt(z3, p["conv_w"], ts=t["conv_ts"], tc=t["conv_tc"])
    y_gla, gla_new = gla_prompt(z3, p["wa2"], p["ba"], p["onorm"], ts=t["gla_ts"])
    q, kv32, kvb = mla_prep(z3, rope_tab[0], rope_tab[1], p["q_norm"], p["kv_norm"], p["wuq"], p["wuk_t"],
                            tm=t["prep_tm"])
    y_mla = mla_prompt_attend(q, kvb, p["wuv_t"], tq=t["attn_tq"], tk=t["attn_tk"], hb=t["attn_heads_per_block"])
    ys = [y.reshape(bsz * seq, BRANCH_DIM) for y in (y_conv, y_gla, y_mla)]
    x2 = _finish_layer(x2, z, ys, p, tm=t["row_tm"])
    return x2, kv32, gla_new, conv_new


def _sample_layer(x2, bsz, n_tok, rope_tab, conv_buf, gla_s0, cache_t, page_table, p):
    t = SAMPLE_TILES
    m = bsz * n_tok
    z = norm_matmul(x2, p["norms"][0], p["w_in_t"], p["layer"], tm=m, tn=NZ_TILE)
    buf_rows = jnp.repeat(conv_buf.reshape(bsz, (CONV_WIDTH - 1) * CONV_DIM), n_tok, axis=0)
    y_conv, u = conv_sample(z, buf_rows, p["conv_w"], n_tok=n_tok, tc=t["conv_tc"])
    conv_new = u.reshape(bsz, n_tok, CONV_DIM)[:, n_tok - (CONV_WIDTH - 1):, :]
    y_gla, gla_new = gla_sample(z, gla_s0, p["layer"], p["wa2"], p["ba"], p["onorm"], n_tok=n_tok, tb=t["gla_tb"])
    q, kv32, kvb = mla_prep(z.reshape(1, m, NZ), rope_tab[0], rope_tab[1], p["q_norm"], p["kv_norm"],
                            p["wuq"], p["wuk_t"], tm=t["prep_tm"])
    qs = q.reshape(MLA_HEADS, bsz, n_tok, MLA_QK).transpose(1, 0, 2, 3).reshape(bsz, MLA_HEADS * n_tok, MLA_QK)
    kt_new = jnp.pad(jnp.swapaxes(kvb.reshape(bsz, n_tok, MLA_QK), 1, 2), ((0, 0), (0, 0), (0, PAGE_SIZE - n_tok)))
    o_lat = mla_sample_attend(qs, kt_new, cache_t, page_table, layer=p["layer"], n_tok=n_tok,
                              cp=t["attn_pages_per_chunk"])
    o_h = o_lat.reshape(bsz, MLA_HEADS, n_tok, MLA_KV_RANK).transpose(1, 0, 2, 3).reshape(MLA_HEADS, m, MLA_KV_RANK)
    y_mla = head_matmul(o_h, p["wuv_t"])
    x2 = _finish_layer(x2, z, [y_conv, y_gla, y_mla], p, tm=m)
    return x2, kv32.reshape(bsz, n_tok, MLA_QK), gla_new, conv_new


def kernel(x_prompt, x_sample, cache_mla, page_table, state_gla, state_conv, norms, w_in, conv_w, gla_wa2, gla_ba, gla_onorm, mla_q_norm, mla_kv_norm, mla_wuq, mla_wuk, mla_wuv, w_branch, w_out, ffn_w_gu, ffn_w_down):
    bp, sp, d = x_prompt.shape
    bs, ss, _ = x_sample.shape
    depth = norms.shape[0]
    past_len = page_table.shape[1] * PAGE_SIZE
    tab_p = _rope_tables(np.arange(sp), MLA_HEADS)
    tab_s = _rope_tables(past_len + np.arange(bs * ss) % ss, MLA_HEADS)
    xp = x_prompt.reshape(bp * sp, d)
    xs = x_sample.reshape(bs * ss, d)
    cache_t = jnp.swapaxes(cache_mla, 2, 3)
    big = _stacked_weights(w_in, w_branch, w_out, ffn_w_gu, ffn_w_down)
    outs = [[] for _ in range(6)]
    for l in range(depth):
        p = _layer_weights(l, big, norms, conv_w, gla_wa2, gla_ba, gla_onorm, mla_q_norm, mla_kv_norm, mla_wuq,
                           mla_wuk, mla_wuv)
        xp, kv_p, gla_p, conv_p = _prompt_layer(xp, bp, sp, tab_p, p)
        xs, kv_s, gla_s, conv_s = _sample_layer(xs, bs, ss, tab_s, state_conv[l], state_gla, cache_t, page_table, p)
        for acc, val in zip(outs, (kv_p, kv_s, gla_p, gla_s, conv_p, conv_s)):
            acc.append(val)
    return (xp.reshape(bp, sp, d), xs.reshape(bs, ss, d)) + tuple(jnp.stack(o) for o in outs)
```

```python
import functools
import math

import numpy as np
import jax
import jax.numpy as jnp
from jax import lax
from jax.experimental import pallas as pl
from jax.experimental.pallas import tpu as pltpu

F32 = jnp.float32
BF16 = jnp.bfloat16

D_MODEL = 2048
BRANCH_DIM = 1024
N_BRANCH = 3
CONV_DIM = BRANCH_DIM
CONV_WIDTH = 3
GLA_HEADS = 4
GLA_DK = 128
GLA_DV = BRANCH_DIM // GLA_HEADS
GLA_LOWRANK = 16
GLA_TAU = 16.0
GLA_CHUNK = 64
MLA_HEADS = 8
MLA_Q_RANK = 512
MLA_KV_RANK = 256
MLA_NOPE = 128
MLA_ROPE = 64
MLA_V = BRANCH_DIM // MLA_HEADS
MLA_SCALE = (MLA_NOPE + MLA_ROPE) ** -0.5
MLA_QK = MLA_KV_RANK + MLA_ROPE
ROPE_THETA = 10000.0
PAGE_SIZE = 128
D_FF = -(-8 * D_MODEL // (3 * 256)) * 256
RMS_EPS = 1e-6
NEG_INF = -1e30

LANES = 128
SUBLANES = 8
VMEM_LIMIT_BYTES = 56 * 1024 * 1024

Z_GATES = 0
Z_CONV = Z_GATES + N_BRANCH * D_MODEL
Z_GQ = Z_CONV + 3 * CONV_DIM
Z_GK = Z_GQ + GLA_HEADS * GLA_DK
Z_GV = Z_GK + GLA_HEADS * GLA_DK
Z_GG = Z_GV + GLA_HEADS * GLA_DV
Z_MQ = Z_GG + GLA_HEADS * GLA_DV
Z_MKV = Z_MQ + MLA_Q_RANK
Z_MISC = Z_MKV + MLA_KV_RANK
Z_END = Z_MISC + LANES
NZ_TILE = 512
NZ = -(-Z_END // NZ_TILE) * NZ_TILE
MISC_KR = 0
MISC_GA = MLA_ROPE


def _cparams(*sem):
    return pltpu.CompilerParams(dimension_semantics=sem, vmem_limit_bytes=VMEM_LIMIT_BYTES)


def _rms_rows(x, g):
    ms = jnp.mean(x * x, axis=-1, keepdims=True)
    return x * lax.rsqrt(ms + RMS_EPS) * g


def _norm_rows_to(dst_ref, src_ref, g_ref, rows, chunk=32):
    def body(c, carry):
        r0 = pl.multiple_of(c * chunk, chunk)
        dst_ref[pl.ds(r0, chunk), :] = _rms_rows(src_ref[pl.ds(r0, chunk), :], g_ref[...]).astype(dst_ref.dtype)
        return carry
    lax.fori_loop(0, rows // chunk, body, 0)


def _norm_mm_kernel(x_ref, g_ref, w_ref, o_ref, xn_ref):
    @pl.when(pl.program_id(1) == 0)
    def _():
        _norm_rows_to(xn_ref, x_ref, g_ref, x_ref.shape[0])
    o_ref[...] = _dot_nt(xn_ref[...], w_ref[...])


def norm_matmul(x, g, w_t, layer, *, tm, tn):
    m, k = x.shape
    n = w_t.shape[1]
    return pl.pallas_call(
        _norm_mm_kernel,
        grid=(m // tm, n // tn),
        in_specs=[pl.BlockSpec((tm, k), lambda i, j: (i, 0)),
                  pl.BlockSpec((1, k), lambda i, j: (0, 0)),
                  pl.BlockSpec((None, tn, k), lambda i, j: (layer, j, 0))],
        out_specs=pl.BlockSpec((tm, tn), lambda i, j: (i, j)),
        out_shape=jax.ShapeDtypeStruct((m, n), F32),
        scratch_shapes=[pltpu.VMEM((tm, k), BF16)],
        compiler_params=_cparams("parallel", "arbitrary"),
        name="in_proj",
    )(x, g.reshape(1, k), w_t)


def _conv_prompt_kernel(cb_ref, cc_ref, ch_ref, w_ref, y_ref, st_ref, carry_ref):
    i = pl.program_id(2)

    @pl.when(i == 0)
    def _():
        carry_ref[...] = jnp.zeros_like(carry_ref)

    u = cc_ref[...] * ch_ref[...]
    ts = u.shape[0]
    row = lax.broadcasted_iota(jnp.int32, u.shape, 0)
    prev1 = carry_ref[1:2, :]
    prev2 = carry_ref[0:1, :]
    um1 = jnp.where(row == 0, prev1, pltpu.roll(u, 1, axis=0))
    um2 = jnp.where(row == 0, prev2, jnp.where(row == 1, prev1, pltpu.roll(u, 2, axis=0)))
    w = w_ref[...]
    y = um2 * w[0:1, :] + um1 * w[1:2, :] + u * w[2:3, :]
    y_ref[...] = (cb_ref[...] * y).astype(y_ref.dtype)
    last = u[ts - 2:ts, :]
    carry_ref[...] = last

    @pl.when(i == pl.num_programs(2) - 1)
    def _():
        st_ref[...] = last


def conv_prompt(z3, conv_w, *, ts, tc):
    b, s, _ = z3.shape
    c = CONV_DIM
    base = Z_CONV // tc
    nc = c // tc

    def zspec(part):
        return pl.BlockSpec((None, ts, tc), lambda bi, ci, i, part=part: (bi, i, base + part * nc + ci))

    return pl.pallas_call(
        _conv_prompt_kernel,
        grid=(b, nc, s // ts),
        in_specs=[zspec(0), zspec(1), zspec(2),
                  pl.BlockSpec((CONV_WIDTH, tc), lambda bi, ci, i: (0, ci))],
        out_specs=[pl.BlockSpec((None, ts, tc), lambda bi, ci, i: (bi, i, ci)),
                   pl.BlockSpec((None, CONV_WIDTH - 1, tc), lambda bi, ci, i: (bi, 0, ci))],
        out_shape=[jax.ShapeDtypeStruct((b, s, c), BF16),
                   jax.ShapeDtypeStruct((b, CONV_WIDTH - 1, c), F32)],
        scratch_shapes=[pltpu.VMEM((CONV_WIDTH - 1, tc), F32)],
        compiler_params=_cparams("parallel", "parallel", "arbitrary"),
        name="conv_prompt",
    )(z3, z3, z3, conv_w)


def _conv_sample_kernel(cb_ref, cc_ref, ch_ref, b0_ref, b1_ref, w_ref, y_ref, u_ref, *, n_tok):
    u = cc_ref[...] * ch_ref[...]
    tpos = lax.broadcasted_iota(jnp.int32, u.shape, 0) % n_tok
    prev1 = b1_ref[...]
    um1 = jnp.where(tpos == 0, prev1, pltpu.roll(u, 1, axis=0))
    um2 = jnp.where(tpos == 0, b0_ref[...], jnp.where(tpos == 1, prev1, pltpu.roll(u, 2, axis=0)))
    w = w_ref[...]
    y = um2 * w[0:1, :] + um1 * w[1:2, :] + u * w[2:3, :]
    y_ref[...] = cb_ref[...] * y
    u_ref[...] = u


def conv_sample(z2, buf_rows, conv_w, *, n_tok, tc):
    m = z2.shape[0]
    c = CONV_DIM
    nc = c // tc
    base = Z_CONV // tc

    def zspec(part):
        return pl.BlockSpec((m, tc), lambda ci, part=part: (0, base + part * nc + ci))

    return pl.pallas_call(
        functools.partial(_conv_sample_kernel, n_tok=n_tok),
        grid=(nc,),
        in_specs=[zspec(0), zspec(1), zspec(2),
                  pl.BlockSpec((m, tc), lambda ci: (0, ci)),
                  pl.BlockSpec((m, tc), lambda ci: (0, nc + ci)),
                  pl.BlockSpec((CONV_WIDTH, tc), lambda ci: (0, ci))],
        out_specs=[pl.BlockSpec((m, tc), lambda ci: (0, ci)), pl.BlockSpec((m, tc), lambda ci: (0, ci))],
        out_shape=[jax.ShapeDtypeStruct((m, c), F32), jax.ShapeDtypeStruct((m, c), F32)],
        compiler_params=_cparams("parallel"),
        name="conv_sample",
    )(z2, z2, z2, buf_rows, buf_rows, conv_w)


def _log_sigmoid(x):
    return jnp.minimum(x, 0.0) - jnp.log1p(jnp.exp(-jnp.abs(x)))


def _split3(x):
    hi = x.astype(BF16)
    r1 = x - hi.astype(F32)
    mid = r1.astype(BF16)
    lo = (r1 - mid.astype(F32)).astype(BF16)
    return hi, mid, lo


def _col_bcast(row, width):
    n = row.shape[1]
    sq = jnp.transpose(jnp.broadcast_to(row, (n, n)))
    return jnp.concatenate([sq] * (width // n), axis=1)


def _dot_nt(a, b):
    return lax.dot_general(a, b, (((1,), (1,)), ((), ())), preferred_element_type=F32)


def _dot_tn(a, b):
    return lax.dot_general(a, b, (((0,), (0,)), ((), ())), preferred_element_type=F32)


def _gla_gate_norm(o, g, onorm):
    on = o * lax.rsqrt(jnp.mean(o * o, axis=-1, keepdims=True) + RMS_EPS) * onorm
    return on * (g * jax.nn.sigmoid(g))


def _pad_rows(x, rows):
    if x.shape[0] == rows:
        return x
    return jnp.concatenate([x, jnp.zeros((rows - x.shape[0], x.shape[1]), x.dtype)], axis=0)


def _gla_log_decay(misc, wa2_ref, ba_ref):
    lane = lax.broadcasted_iota(jnp.int32, misc.shape, 1)
    a_low = jnp.where((lane >= MISC_GA) & (lane < MISC_GA + GLA_LOWRANK), misc, 0.0).astype(BF16)
    a = jnp.dot(a_low, wa2_ref[...], preferred_element_type=F32) + ba_ref[...]
    return _log_sigmoid(a) * (1.0 / GLA_TAU)


def _gla_prompt_kernel(q_ref, k_ref, v_ref, g_ref, a_ref, wa2_ref, ba_ref, on_ref, y_ref, st_ref, s_ref, *, ts):
    i = pl.program_id(1)

    @pl.when(i == 0)
    def _():
        s_ref[...] = jnp.zeros_like(s_ref)

    c = GLA_CHUNK
    pc = LANES
    rr = lax.broadcasted_iota(jnp.int32, (c, pc), 0)
    cc = lax.broadcasted_iota(jnp.int32, (c, pc), 1)
    tril = rr >= cc
    tril_b = tril.astype(BF16)
    scale = GLA_DK ** -0.5
    for ci in range(ts // c):
        r = slice(ci * c, (ci + 1) * c)
        la = _gla_log_decay(a_ref[r, :], wa2_ref, ba_ref)
        big_l = None
        for term in _split3(_pad_rows(la, pc)):
            part = jnp.dot(tril_b, term, preferred_element_type=F32)
            big_l = part if big_l is None else big_l + part
        l_last = big_l[c - 1:c, :]
        e_pos = jnp.exp(big_l)
        e_neg = jnp.exp(-big_l)
        e_dec = jnp.exp(l_last - big_l)
        e_last = jnp.exp(l_last)
        heads = range(GLA_HEADS)
        hs = [slice(h * GLA_DK, (h + 1) * GLA_DK) for h in heads]
        vs = [slice(h * GLA_DV, (h + 1) * GLA_DV) for h in heads]
        ks = [k_ref[r, hs[h]] for h in heads]
        q_t = [(q_ref[r, hs[h]] * scale * e_pos[:, hs[h]]).astype(BF16) for h in heads]
        attn_raw = [_dot_nt(q_t[h], _pad_rows(ks[h] * e_neg[:, hs[h]], pc).astype(BF16)) for h in heads]
        v = [_pad_rows(v_ref[r, vs[h]], pc).astype(BF16) for h in heads]
        s_old = [s_ref[h] for h in heads]
        o_inter = [jnp.dot(q_t[h], s_old[h].astype(BF16), preferred_element_type=F32) for h in heads]
        s_add = [_dot_tn(_pad_rows(ks[h] * e_dec[:, hs[h]], pc).astype(BF16), v[h]) for h in heads]
        for h in heads:
            attn = jnp.where(tril, attn_raw[h], 0.0).astype(BF16)
            o = o_inter[h] + jnp.dot(attn, v[h], preferred_element_type=F32)
            s_ref[h] = s_old[h] * _col_bcast(e_last[:, hs[h]], GLA_DV) + s_add[h]
            y_ref[r, vs[h]] = _gla_gate_norm(o, g_ref[r, vs[h]], on_ref[...]).astype(y_ref.dtype)

    @pl.when(i == pl.num_programs(1) - 1)
    def _():
        st_ref[...] = s_ref[...]


def gla_prompt(z3, wa2, ba, onorm, *, ts):
    b, s, _ = z3.shape
    hk = GLA_HEADS * GLA_DK
    hv = GLA_HEADS * GLA_DV
    return pl.pallas_call(
        functools.partial(_gla_prompt_kernel, ts=ts),
        grid=(b, s // ts),
        in_specs=[pl.BlockSpec((None, ts, hk), lambda bi, i: (bi, i, Z_GQ // hk)),
                  pl.BlockSpec((None, ts, hk), lambda bi, i: (bi, i, Z_GK // hk)),
                  pl.BlockSpec((None, ts, hv), lambda bi, i: (bi, i, Z_GV // hv)),
                  pl.BlockSpec((None, ts, hv), lambda bi, i: (bi, i, Z_GG // hv)),
                  pl.BlockSpec((None, ts, LANES), lambda bi, i: (bi, i, Z_MISC // LANES)),
                  pl.BlockSpec((LANES, hk), lambda bi, i: (0, 0)),
                  pl.BlockSpec((1, hk), lambda bi, i: (0, 0)),
                  pl.BlockSpec((1, GLA_DV), lambda bi, i: (0, 0))],
        out_specs=[pl.BlockSpec((None, ts, hv), lambda bi, i: (bi, i, 0)),
                   pl.BlockSpec((None, GLA_HEADS, GLA_DK, GLA_DV), lambda bi, i: (bi, 0, 0, 0))],
        out_shape=[jax.ShapeDtypeStruct((b, s, hv), BF16),
                   jax.ShapeDtypeStruct((b, GLA_HEADS, GLA_DK, GLA_DV), F32)],
        scratch_shapes=[pltpu.VMEM((GLA_HEADS, GLA_DK, GLA_DV), F32)],
        compiler_params=_cparams("parallel", "arbitrary"),
        name="gla_prompt",
    )(z3, z3, z3, z3, z3, wa2, ba.reshape(1, hk), onorm.reshape(1, GLA_DV))


def _gla_sample_kernel(q_ref, k_ref, v_ref, g_ref, a_ref, wa2_ref, ba_ref, on_ref, s0_ref, y_ref, st_ref,
                       *, n_tok, n_groups):
    rows = SUBLANES
    per = rows // n_tok
    pr = 2 * SUBLANES
    pc = LANES
    rr = lax.broadcasted_iota(jnp.int32, (pr, pc), 0)
    cc = lax.broadcasted_iota(jnp.int32, (pr, pc), 1)
    causal = (rr // n_tok == cc // n_tok) & (rr >= cc)
    trow = lax.broadcasted_iota(jnp.int32, (rows, 1), 0)
    tpos = trow % n_tok
    scale = GLA_DK ** -0.5

    def body(p, carry):
        r0 = pl.multiple_of(p * rows, rows)
        r = pl.ds(r0, rows)
        la = _gla_log_decay(a_ref[r, :], wa2_ref, ba_ref)
        big_l = la
        for d in range(1, n_tok):
            big_l = big_l + jnp.where(tpos >= d, pltpu.roll(la, d, axis=0), 0.0)
        l_last = jnp.zeros_like(big_l)
        for j in range(per):
            last_j = big_l[(j + 1) * n_tok - 1:(j + 1) * n_tok, :]
            l_last = jnp.where(trow // n_tok == j, last_j, l_last)
        e_pos = jnp.exp(big_l)
        e_neg = jnp.exp(-big_l)
        e_dec = jnp.exp(l_last - big_l)
        for h in range(GLA_HEADS):
            hs = slice(h * GLA_DK, (h + 1) * GLA_DK)
            vs = slice(h * GLA_DV, (h + 1) * GLA_DV)
            k = k_ref[r, hs]
            q_t = _pad_rows(q_ref[r, hs] * scale * e_pos[:, hs], pr).astype(BF16)
            k_t = _pad_rows(k * e_neg[:, hs], pc).astype(BF16)
            k_dec = k * e_dec[:, hs]
            v = _pad_rows(v_ref[r, vs], pc).astype(BF16)
            attn = jnp.where(causal, _dot_nt(q_t, k_t), 0.0).astype(BF16)
            o = jnp.dot(attn, v, preferred_element_type=F32)[0:rows]
            for j in range(per):
                seq = p * per + j
                in_seq = trow // n_tok == j
                s_old = s0_ref[seq, h]
                o_inter = jnp.dot(q_t, s_old.astype(BF16), preferred_element_type=F32)[0:rows]
                o = o + jnp.where(in_seq, o_inter, 0.0)
                e_last = jnp.exp(big_l[(j + 1) * n_tok - 1:(j + 1) * n_tok, hs])
                k_dec_j = _pad_rows(jnp.where(in_seq, k_dec, 0.0), pc).astype(BF16)
                st_ref[seq, h] = s_old * _col_bcast(e_last, GLA_DV) + _dot_tn(k_dec_j, v)
            y_ref[r, vs] = _gla_gate_norm(o, g_ref[r, vs], on_ref[...])
        return carry

    lax.fori_loop(0, n_groups, body, 0)


def gla_sample(z2, s0_all, layer, wa2, ba, onorm, *, n_tok, tb):
    m = z2.shape[0]
    b = m // n_tok
    hk = GLA_HEADS * GLA_DK
    hv = GLA_HEADS * GLA_DV
    tr = tb * n_tok
    st_spec = pl.BlockSpec((tb, GLA_HEADS, GLA_DK, GLA_DV), lambda i: (i, 0, 0, 0))
    s0_spec = pl.BlockSpec((None, tb, GLA_HEADS, GLA_DK, GLA_DV), lambda i: (layer, i, 0, 0, 0))
    return pl.pallas_call(
        functools.partial(_gla_sample_kernel, n_tok=n_tok, n_groups=tr // SUBLANES),
        grid=(b // tb,),
        in_specs=[pl.BlockSpec((tr, hk), lambda i: (i, Z_GQ // hk)),
                  pl.BlockSpec((tr, hk), lambda i: (i, Z_GK // hk)),
                  pl.BlockSpec((tr, hv), lambda i: (i, Z_GV // hv)),
                  pl.BlockSpec((tr, hv), lambda i: (i, Z_GG // hv)),
                  pl.BlockSpec((tr, LANES), lambda i: (i, Z_MISC // LANES)),
                  pl.BlockSpec((LANES, hk), lambda i: (0, 0)),
                  pl.BlockSpec((1, hk), lambda i: (0, 0)),
                  pl.BlockSpec((1, GLA_DV), lambda i: (0, 0)),
                  s0_spec],
        out_specs=[pl.BlockSpec((tr, hv), lambda i: (i, 0)), st_spec],
        out_shape=[jax.ShapeDtypeStruct((m, hv), F32),
                   jax.ShapeDtypeStruct(s0_all.shape[1:], F32)],
        compiler_params=_cparams("parallel"),
        name="gla_sample",
    )(z2, z2, z2, z2, z2, wa2, ba.reshape(1, hk), onorm.reshape(1, GLA_DV), s0_all)


def _rope_tables(pos, reps):
    half = MLA_ROPE // 2
    inv_freq = ROPE_THETA ** (-np.arange(half, dtype=np.float64) / half)
    ang = np.asarray(pos, np.float64)[:, None] * inv_freq[None, :]
    cos = np.concatenate([np.cos(ang), np.cos(ang)], axis=1)
    sin = np.concatenate([-np.sin(ang), np.sin(ang)], axis=1)
    return (jnp.asarray(np.tile(cos, (1, reps)), F32), jnp.asarray(np.tile(sin, (1, reps)), F32))


def _rope(x, cos, sin):
    n = x.shape[1]
    half = MLA_ROPE // 2
    lane = lax.broadcasted_iota(jnp.int32, x.shape, 1)
    partner = jnp.where(lane % MLA_ROPE < half, pltpu.roll(x, n - half, axis=1), pltpu.roll(x, half, axis=1))
    return x * cos + partner * sin


def _mla_prep_kernel(mq_ref, mkv_ref, misc_ref, cos_ref, sin_ref, qn_ref, kvn_ref, wuq_ref, wuk_ref,
                     q_out, kv32_out, kvb_out, cq_ref, q_ref):
    tm = mq_ref.shape[0]
    _norm_rows_to(cq_ref, mq_ref, qn_ref, tm)
    q_ref[...] = jnp.dot(cq_ref[...], wuq_ref[...], preferred_element_type=F32)
    n_nope = MLA_HEADS * MLA_NOPE
    q_rope = _rope(q_ref[:, n_nope:], cos_ref[...], sin_ref[...])
    for h in range(MLA_HEADS):
        q_nope = q_ref[:, h * MLA_NOPE:(h + 1) * MLA_NOPE].astype(BF16)
        q_out[h, :, 0:MLA_KV_RANK] = jnp.dot(q_nope, wuk_ref[h], preferred_element_type=F32).astype(BF16)
        q_out[h, :, MLA_KV_RANK:MLA_QK] = q_rope[:, h * MLA_ROPE:(h + 1) * MLA_ROPE].astype(BF16)
    c = _rms_rows(mkv_ref[...], kvn_ref[...])
    kr = _rope(misc_ref[...], cos_ref[:, 0:LANES], sin_ref[:, 0:LANES])[:, MISC_KR:MISC_KR + MLA_ROPE]
    kv32_out[:, 0:MLA_KV_RANK] = c
    kv32_out[:, MLA_KV_RANK:MLA_QK] = kr
    kvb_out[:, 0:MLA_KV_RANK] = c.astype(BF16)
    kvb_out[:, MLA_KV_RANK:MLA_QK] = kr.astype(BF16)


def mla_prep(z3, cos, sin, q_norm, kv_norm, wuq, wuk_t, *, tm):
    b, s, _ = z3.shape
    nq = MLA_HEADS * (MLA_NOPE + MLA_ROPE)
    nr = MLA_HEADS * MLA_ROPE
    const2 = lambda bi, i: (0, 0)
    return pl.pallas_call(
        _mla_prep_kernel,
        grid=(b, s // tm),
        in_specs=[pl.BlockSpec((None, tm, MLA_Q_RANK), lambda bi, i: (bi, i, Z_MQ // MLA_Q_RANK)),
                  pl.BlockSpec((None, tm, MLA_KV_RANK), lambda bi, i: (bi, i, Z_MKV // MLA_KV_RANK)),
                  pl.BlockSpec((None, tm, LANES), lambda bi, i: (bi, i, Z_MISC // LANES)),
                  pl.BlockSpec((tm, nr), lambda bi, i: (i, 0)),
                  pl.BlockSpec((tm, nr), lambda bi, i: (i, 0)),
                  pl.BlockSpec((1, MLA_Q_RANK), const2),
                  pl.BlockSpec((1, MLA_KV_RANK), const2),
                  pl.BlockSpec((MLA_Q_RANK, nq), const2),
                  pl.BlockSpec((MLA_HEADS, MLA_NOPE, MLA_KV_RANK), lambda bi, i: (0, 0, 0))],
        out_specs=[pl.BlockSpec((None, MLA_HEADS, tm, MLA_QK), lambda bi, i: (bi, 0, i, 0)),
                   pl.BlockSpec((None, tm, MLA_QK), lambda bi, i: (bi, i, 0)),
                   pl.BlockSpec((None, tm, MLA_QK), lambda bi, i: (bi, i, 0))],
        out_shape=[jax.ShapeDtypeStruct((b, MLA_HEADS, s, MLA_QK), BF16),
                   jax.ShapeDtypeStruct((b, s, MLA_QK), F32),
                   jax.ShapeDtypeStruct((b, s, MLA_QK), BF16)],
        scratch_shapes=[pltpu.VMEM((tm, MLA_Q_RANK), BF16), pltpu.VMEM((tm, nq), F32)],
        compiler_params=_cparams("parallel", "parallel"),
        name="mla_prep",
    )(z3, z3, z3, cos, sin, q_norm.reshape(1, -1), kv_norm.reshape(1, -1), wuq, wuk_t)


SOFTMAX_EXP2_SCALE = MLA_SCALE * math.log2(math.e)


def _online_softmax(s, kv_c_t, m_prev, l_prev, acc_prev, contract_kv_lanes):
    m_new = jnp.maximum(m_prev, jnp.max(s, axis=-1, keepdims=True))
    corr = jnp.exp2((m_prev - m_new) * SOFTMAX_EXP2_SCALE)
    p = jnp.exp2((s - m_new) * SOFTMAX_EXP2_SCALE)
    pv = _dot_nt(p.astype(BF16), kv_c_t) if contract_kv_lanes else jnp.dot(
        p.astype(BF16), kv_c_t, preferred_element_type=F32)
    return m_new, l_prev * corr + jnp.sum(p, axis=-1, keepdims=True), acc_prev * corr + pv


def _mla_prompt_kernel(q_ref, kv_ref, wuv_ref, y_ref, m_ref, l_ref, acc_ref,
                       s0_ref, s1_ref, p0_ref, p1_ref, c0_ref, c1_ref, *, tq, tk, hb):
    i = pl.program_id(1)
    rb = hb * tq
    n_blocks = MLA_HEADS // hb
    shift = n_blocks.bit_length() - 1
    assert n_blocks == 1 << shift
    m_ref[...] = jnp.full_like(m_ref, NEG_INF)
    l_ref[...] = jnp.zeros_like(l_ref)
    acc_ref[...] = jnp.zeros_like(acc_ref)
    n_items = ((i * tq + tq - 1) // tk + 1) * n_blocks

    def key_rows(t):
        return pl.ds(pl.multiple_of(lax.shift_right_logical(t, shift) * tk, tk), tk)

    def scores(t, s_ref):
        r = t & (n_blocks - 1)
        q = q_ref[pl.ds(r * hb, hb)].reshape(rb, MLA_QK)
        s = _dot_nt(q, kv_ref[key_rows(t), :])
        qpos = i * tq + lax.broadcasted_iota(jnp.int32, s.shape, 0) % tq
        kpos = lax.shift_right_logical(t, shift) * tk + lax.broadcasted_iota(jnp.int32, s.shape, 1)
        s_ref[...] = jnp.where(kpos <= qpos, s, NEG_INF)

    def softmax_stats(t, s_ref, p_ref, c_ref):
        r = t & (n_blocks - 1)
        s = s_ref[...]
        m_prev = m_ref[r]
        m_new = jnp.maximum(m_prev, jnp.max(s, axis=-1, keepdims=True))
        corr = jnp.exp2((m_prev - m_new) * SOFTMAX_EXP2_SCALE)
        p = jnp.exp2((s - m_new) * SOFTMAX_EXP2_SCALE)
        l_ref[r] = l_ref[r] * corr + jnp.sum(p, axis=-1, keepdims=True)
        m_ref[r] = m_new
        p_ref[...] = p.astype(BF16)
        c_ref[...] = corr

    def accumulate(t, p_ref, c_ref):
        r = t & (n_blocks - 1)
        kv_c = kv_ref[key_rows(t), 0:MLA_KV_RANK]
        acc_ref[r] = acc_ref[r] * c_ref[...] + jnp.dot(p_ref[...], kv_c, preferred_element_type=F32)

    assert n_blocks % 2 == 0 and n_blocks >= 4
    buf0, buf1 = (s0_ref, p0_ref, c0_ref), (s1_ref, p1_ref, c1_ref)

    def step(t, cur, nxt, with_scores=True):
        if with_scores:
            scores(t + 1, nxt[0])
        softmax_stats(t, *cur)
        accumulate(t - 1, nxt[1], nxt[2])

    def step_pair(u, carry):
        t = 2 * u + 1
        step(t, buf1, buf0)
        step(t + 1, buf0, buf1)
        return carry

    scores(jnp.int32(0), s0_ref)
    scores(jnp.int32(1), s1_ref)
    softmax_stats(jnp.int32(0), *buf0)
    lax.fori_loop(0, lax.shift_right_logical(n_items, 1) - 1, step_pair, 0)
    step(n_items - 1, buf1, buf0, with_scores=False)
    accumulate(n_items - 1, p1_ref, c1_ref)

    for h in range(MLA_HEADS):
        r, hs = h // hb, slice((h % hb) * tq, (h % hb + 1) * tq)
        o = (acc_ref[r, hs, :] / l_ref[r, hs, :]).astype(BF16)
        y_ref[:, h * MLA_V:(h + 1) * MLA_V] = jnp.dot(o, wuv_ref[h], preferred_element_type=F32).astype(y_ref.dtype)


def mla_prompt_attend(q, kvb, wuv_t, *, tq, tk, hb):
    b, _, s, _ = q.shape
    rb = hb * tq
    n_blocks = MLA_HEADS // hb
    return pl.pallas_call(
        functools.partial(_mla_prompt_kernel, tq=tq, tk=tk, hb=hb),
        grid=(b, s // tq),
        in_specs=[pl.BlockSpec((None, MLA_HEADS, tq, MLA_QK), lambda bi, i: (bi, 0, i, 0)),
                  pl.BlockSpec((None, s, MLA_QK), lambda bi, i: (bi, 0, 0)),
                  pl.BlockSpec((MLA_HEADS, MLA_KV_RANK, MLA_V), lambda bi, i: (0, 0, 0))],
        out_specs=pl.BlockSpec((None, tq, BRANCH_DIM), lambda bi, i: (bi, i, 0)),
        out_shape=jax.ShapeDtypeStruct((b, s, BRANCH_DIM), BF16),
        scratch_shapes=[pltpu.VMEM((n_blocks, rb, 1), F32), pltpu.VMEM((n_blocks, rb, 1), F32),
                        pltpu.VMEM((n_blocks, rb, MLA_KV_RANK), F32),
                        pltpu.VMEM((rb, tk), F32), pltpu.VMEM((rb, tk), F32),
                        pltpu.VMEM((rb, tk), BF16), pltpu.VMEM((rb, tk), BF16),
                        pltpu.VMEM((rb, 1), F32), pltpu.VMEM((rb, 1), F32)],
        compiler_params=_cparams("parallel", "arbitrary"),
        name="mla_prompt_attn",
    )(q, kvb, wuv_t)


def _lane_row(col):
    rows = col.shape[0]
    sq = jnp.transpose(jnp.broadcast_to(_pad_rows(col, LANES), (LANES, LANES)))
    return sq[0:1, 0:rows]


def _absorb_keys(s, kt_c, carry):
    m_prev, l_prev, acc_t = carry
    m_new = jnp.maximum(m_prev, jnp.max(s, axis=-1, keepdims=True))
    corr = jnp.exp2((m_prev - m_new) * SOFTMAX_EXP2_SCALE)
    p = jnp.exp2((s - m_new) * SOFTMAX_EXP2_SCALE)
    l_new = l_prev * corr + jnp.sum(p, axis=-1, keepdims=True)
    return m_new, l_new, acc_t * _lane_row(corr) + _dot_nt(kt_c, p.astype(BF16))


def _mla_sample_kernel(pt_ref, q_ref, new_ref, cache_ref, o_ref, f0, f1, h0, h1, s0, s1, sem,
                       *, layer, n_tok, n_chunks, cp):
    b = pl.program_id(0)
    nb = pl.num_programs(0)
    fbuf, hbuf, sbuf = (f0, f1), (h0, h1), (s0, s1)

    def page_copy(bi, c, p, slot):
        page = pt_ref[bi, c * cp + p]
        return pltpu.make_async_copy(cache_ref.at[layer, page], fbuf[slot].at[p], sem.at[slot])

    def start_chunk(bi, c, slot):
        for p in range(cp):
            page_copy(bi, c, p, slot).start(priority=p % 2)

    def start_ahead(c, slot):
        @pl.when(c < n_chunks)
        def _():
            start_chunk(b, c, slot)

        @pl.when((c >= n_chunks) & (b + 1 < nb))
        def _():
            start_chunk(b + 1, c - n_chunks, slot)

    def score_chunk(c, slot):
        for p in range(cp):
            page_copy(b, c, p, slot).wait()
        for p in range(cp):
            hbuf[slot][:, p * PAGE_SIZE:(p + 1) * PAGE_SIZE] = fbuf[slot][p].astype(BF16)
        sbuf[slot][...] = jnp.dot(q, hbuf[slot][...], preferred_element_type=F32)

    def absorb_chunk(slot, carry):
        return _absorb_keys(sbuf[slot][...], hbuf[slot][0:MLA_KV_RANK, :], carry)

    @pl.when(b == 0)
    def _():
        start_chunk(b, 0, 0)
        start_chunk(b, 1, 1)

    q = q_ref[...]
    rows = q.shape[0]
    score_chunk(0, 0)

    def chunk_pair(c2, carry):
        c = 2 * c2
        start_ahead(c + 2, 0)
        score_chunk(c + 1, 1)
        carry = absorb_chunk(0, carry)
        start_ahead(c + 3, 1)
        score_chunk(c + 2, 0)
        return absorb_chunk(1, carry)

    carry = (jnp.full((rows, 1), NEG_INF, F32), jnp.zeros((rows, 1), F32), jnp.zeros((MLA_KV_RANK, rows), F32))
    carry = lax.fori_loop(0, n_chunks // 2 - 1, chunk_pair, carry)
    start_ahead(n_chunks, 0)
    score_chunk(n_chunks - 1, 1)
    carry = absorb_chunk(0, carry)
    start_ahead(n_chunks + 1, 1)
    carry = absorb_chunk(1, carry)

    kt_new = new_ref[...]
    s = jnp.dot(q, kt_new, preferred_element_type=F32)
    tpos = lax.broadcasted_iota(jnp.int32, s.shape, 0) % n_tok
    kpos = lax.broadcasted_iota(jnp.int32, s.shape, 1)
    m, l, acc_t = _absorb_keys(jnp.where(kpos <= tpos, s, NEG_INF), kt_new[0:MLA_KV_RANK, :], carry)
    o_t = acc_t / _lane_row(l)
    o_sq = jnp.transpose(jnp.concatenate([o_t, jnp.zeros((MLA_KV_RANK, LANES - rows), F32)], axis=1))
    o_ref[...] = o_sq[0:rows, :].astype(o_ref.dtype)


def mla_sample_attend(q, kt_new_pad, cache_t, page_table, *, layer, n_tok, cp):
    b, rows, _ = q.shape
    n_pages = page_table.shape[1]
    n_chunks = n_pages // cp
    keys = cp * PAGE_SIZE
    assert n_chunks % 2 == 0 and n_chunks * cp == n_pages and rows <= LANES
    grid_spec = pltpu.PrefetchScalarGridSpec(
        num_scalar_prefetch=1,
        grid=(b,),
        in_specs=[pl.BlockSpec((None, rows, MLA_QK), lambda bi, pt: (bi, 0, 0)),
                  pl.BlockSpec((None, MLA_QK, PAGE_SIZE), lambda bi, pt: (bi, 0, 0)),
                  pl.BlockSpec(memory_space=pl.ANY)],
        out_specs=pl.BlockSpec((None, rows, MLA_KV_RANK), lambda bi, pt: (bi, 0, 0)),
        scratch_shapes=[pltpu.VMEM((cp, MLA_QK, PAGE_SIZE), F32), pltpu.VMEM((cp, MLA_QK, PAGE_SIZE), F32),
                        pltpu.VMEM((MLA_QK, keys), BF16), pltpu.VMEM((MLA_QK, keys), BF16),
                        pltpu.VMEM((rows, keys), F32), pltpu.VMEM((rows, keys), F32),
                        pltpu.SemaphoreType.DMA((2,))],
    )
    return pl.pallas_call(
        functools.partial(_mla_sample_kernel, layer=layer, n_tok=n_tok, n_chunks=n_chunks, cp=cp),
        grid_spec=grid_spec,
        out_shape=jax.ShapeDtypeStruct((b, rows, MLA_KV_RANK), BF16),
        compiler_params=_cparams("arbitrary"),
        name="mla_sample_attn",
    )(page_table, q, kt_new_pad, cache_t)


def _head_mm_kernel(o_ref, w_ref, y_ref):
    y_ref[...] = jnp.dot(o_ref[...], w_ref[...], preferred_element_type=F32).astype(y_ref.dtype)


def head_matmul(o, w):
    h, m, k = o.shape
    n = w.shape[2]
    return pl.pallas_call(
        _head_mm_kernel,
        grid=(h,),
        in_specs=[pl.BlockSpec((None, m, k), lambda i: (i, 0, 0)),
                  pl.BlockSpec((None, k, n), lambda i: (i, 0, 0))],
        out_specs=pl.BlockSpec((m, n), lambda i: (0, i)),
        out_shape=jax.ShapeDtypeStruct((m, h * n), BF16),
        compiler_params=_cparams("parallel"),
        name="mla_value_up",
    )(o, w)


def _branch_mix_kernel(y0_ref, y1_ref, y2_ref, g0_ref, g1_ref, g2_ref, w_ref, o_ref):
    mix = None
    for n, (y_ref, g_ref) in enumerate(((y0_ref, g0_ref), (y1_ref, g1_ref), (y2_ref, g2_ref))):
        proj = jnp.dot(y_ref[...].astype(BF16), w_ref[n], preferred_element_type=F32)
        term = jax.nn.sigmoid(g_ref[...]) * proj
        mix = term if mix is None else mix + term
    o_ref[...] = mix.astype(o_ref.dtype)


def branch_mix(ys, z, w_branch, layer, *, tm, tn):
    m = z.shape[0]
    per = D_MODEL // tn
    yspec = pl.BlockSpec((tm, BRANCH_DIM), lambda i, j: (i, 0))

    def gspec(n):
        return pl.BlockSpec((tm, tn), lambda i, j, n=n: (i, Z_GATES // tn + n * per + j))

    return pl.pallas_call(
        _branch_mix_kernel,
        grid=(m // tm, per),
        in_specs=[yspec, yspec, yspec, gspec(0), gspec(1), gspec(2),
                  pl.BlockSpec((None, N_BRANCH, BRANCH_DIM, tn), lambda i, j: (layer, 0, 0, j))],
        out_specs=pl.BlockSpec((tm, tn), lambda i, j: (i, j)),
        out_shape=jax.ShapeDtypeStruct((m, D_MODEL), BF16),
        compiler_params=_cparams("parallel", "arbitrary"),
        name="branch_mix",
    )(*ys, z, z, z, w_branch)


def _add_norm_rows(o_ref, x_ref, f_ref, g_ref, rows, chunk=32):
    def body(c, carry):
        r = pl.ds(pl.multiple_of(c * chunk, chunk), chunk)
        o_ref[r, :] = x_ref[r, :] + _rms_rows(f_ref[r, :], g_ref[...])
        return carry
    lax.fori_loop(0, rows // chunk, body, 0)


def _out_proj_kernel(mix_ref, w_ref, x_ref, g_ref, o_ref, f_ref):
    f_ref[...] = jnp.dot(mix_ref[...], w_ref[...], preferred_element_type=F32)
    _add_norm_rows(o_ref, x_ref, f_ref, g_ref, x_ref.shape[0])


def out_proj_residual(mix, w_out, layer, x, g, *, tm):
    m, d = x.shape
    return pl.pallas_call(
        _out_proj_kernel,
        grid=(m // tm,),
        in_specs=[pl.BlockSpec((tm, d), lambda i: (i, 0)),
                  pl.BlockSpec((None, d, d), lambda i: (layer, 0, 0)),
                  pl.BlockSpec((tm, d), lambda i: (i, 0)),
                  pl.BlockSpec((1, d), lambda i: (0, 0))],
        out_specs=pl.BlockSpec((tm, d), lambda i: (i, 0)),
        out_shape=jax.ShapeDtypeStruct((m, d), F32),
        scratch_shapes=[pltpu.VMEM((tm, d), F32)],
        compiler_params=_cparams("parallel"),
        name="out_proj",
    )(mix, w_out, x, g.reshape(1, d))


def _ffn_kernel(x_ref, g_in_ref, wg_ref, wu_ref, wd_ref, g_out_ref, o_ref, xn_ref, acc_ref):
    j = pl.program_id(1)

    @pl.when(j == 0)
    def _():
        _norm_rows_to(xn_ref, x_ref, g_in_ref, x_ref.shape[0])
        acc_ref[...] = jnp.zeros_like(acc_ref)

    xn = xn_ref[...]
    gate = jnp.dot(xn, wg_ref[...], preferred_element_type=F32)
    up = jnp.dot(xn, wu_ref[...], preferred_element_type=F32)
    act = (gate * jax.nn.sigmoid(gate) * up).astype(BF16)
    acc_ref[...] += jnp.dot(act, wd_ref[...], preferred_element_type=F32)

    @pl.when(j == pl.num_programs(1) - 1)
    def _():
        _add_norm_rows(o_ref, x_ref, acc_ref, g_out_ref, x_ref.shape[0])


def ffn(x, g_in, w_gu, w_down, layer, g_out, *, tm, tf):
    m, d = x.shape
    nf = D_FF // tf
    return pl.pallas_call(
        _ffn_kernel,
        grid=(m // tm, nf),
        in_specs=[pl.BlockSpec((tm, d), lambda i, j: (i, 0)),
                  pl.BlockSpec((1, d), lambda i, j: (0, 0)),
                  pl.BlockSpec((None, d, tf), lambda i, j: (layer, 0, j)),
                  pl.BlockSpec((None, d, tf), lambda i, j: (layer, 0, j + nf)),
                  pl.BlockSpec((None, tf, d), lambda i, j: (layer, j, 0)),
                  pl.BlockSpec((1, d), lambda i, j: (0, 0))],
        out_specs=pl.BlockSpec((tm, d), lambda i, j: (i, 0)),
        out_shape=jax.ShapeDtypeStruct((m, d), F32),
        scratch_shapes=[pltpu.VMEM((tm, d), BF16), pltpu.VMEM((tm, d), F32)],
        compiler_params=_cparams("parallel", "arbitrary"),
        name="ffn",
    )(x, g_in.reshape(1, d), w_gu, w_gu, w_down, g_out.reshape(1, d))


def _prep_w_in_t(w_in):
    splits = (CONV_DIM, CONV_DIM, CONV_DIM,
              GLA_HEADS * GLA_DK, GLA_HEADS * GLA_DK, GLA_HEADS * GLA_DV, GLA_LOWRANK, GLA_HEADS * GLA_DV,
              MLA_Q_RANK, MLA_KV_RANK, MLA_ROPE, N_BRANCH * D_MODEL)
    idx = np.cumsum(splits)[:-1].tolist()
    w_t = jnp.swapaxes(w_in, 1, 2)
    cb, cc, ch, gq, gk, gv, ga, gg, mq, mkv, mkr, gates = jnp.split(w_t, idx, axis=1)
    depth, _, d = w_t.shape
    zeros = jnp.zeros((depth, LANES - MLA_ROPE - GLA_LOWRANK + NZ - Z_END, d), w_t.dtype)
    return jnp.concatenate([gates, cb, cc, ch, gq, gk, gv, gg, mq, mkv, mkr, ga, zeros], axis=1).astype(BF16)


def _prep_wuq(wuq):
    w = wuq.reshape(MLA_Q_RANK, MLA_HEADS, MLA_NOPE + MLA_ROPE)
    nope = w[:, :, :MLA_NOPE].reshape(MLA_Q_RANK, MLA_HEADS * MLA_NOPE)
    rope = w[:, :, MLA_NOPE:].reshape(MLA_Q_RANK, MLA_HEADS * MLA_ROPE)
    return jnp.concatenate([nope, rope], axis=1).astype(BF16)


def _stacked_weights(w_in, w_branch, w_out, ffn_w_gu, ffn_w_down):
    return dict(w_in_t=_prep_w_in_t(w_in), w_branch=w_branch.astype(BF16), w_out=w_out.astype(BF16),
                w_gu=ffn_w_gu.astype(BF16), w_down=ffn_w_down.astype(BF16))


def _layer_weights(l, big, norms, conv_w, gla_wa2, gla_ba, gla_onorm, mla_q_norm, mla_kv_norm, mla_wuq,
                   mla_wuk, mla_wuv):
    return dict(
        big, layer=l, norms=norms[l], conv_w=conv_w[l],
        wa2=jnp.pad(gla_wa2[l], ((MISC_GA, LANES - MISC_GA - GLA_LOWRANK), (0, 0))).astype(BF16),
        ba=gla_ba[l], onorm=gla_onorm[l],
        q_norm=mla_q_norm[l], kv_norm=mla_kv_norm[l], wuq=_prep_wuq(mla_wuq[l]),
        wuk_t=jnp.transpose(mla_wuk[l], (1, 2, 0)).astype(BF16),
        wuv_t=jnp.transpose(mla_wuv[l], (1, 0, 2)).astype(BF16))


PROMPT_TILES = dict(in_proj_tm=1024, conv_ts=256, conv_tc=512, gla_ts=256, prep_tm=256,
                    attn_tq=128, attn_tk=512, attn_heads_per_block=2, mix_tm=1024, ffn_tm=512)
SAMPLE_TILES = dict(conv_tc=256, gla_tb=8, prep_tm=256, attn_pages_per_chunk=32)
COL_TILE = 512
OUT_PROJ_TM = 256


def _finish_layer(x2, z, ys, p, *, mix_tm, ffn_tm):
    l = p["layer"]
    mix = branch_mix(ys, z, p["w_branch"], l, tm=mix_tm, tn=COL_TILE)
    x2 = out_proj_residual(mix, p["w_out"], l, x2, p["norms"][1], tm=OUT_PROJ_TM)
    return ffn(x2, p["norms"][2], p["w_gu"], p["w_down"], l, p["norms"][3], tm=ffn_tm, tf=COL_TILE)


def _prompt_layer(x2, bsz, seq, rope_tab, p):
    t = PROMPT_TILES
    z = norm_matmul(x2, p["norms"][0], p["w_in_t"], p["layer"], tm=t["in_proj_tm"], tn=NZ_TILE)
    z3 = z.reshape(bsz, seq, NZ)
    y_conv, conv_new = conv_prompt(z3, p["conv_w"], ts=t["conv_ts"], tc=t["conv_tc"])
    y_gla, gla_new = gla_prompt(z3, p["wa2"], p["ba"], p["onorm"], ts=t["gla_ts"])
    q, kv32, kvb = mla_prep(z3, rope_tab[0], rope_tab[1], p["q_norm"], p["kv_norm"], p["wuq"], p["wuk_t"],
                            tm=t["prep_tm"])
    y_mla = mla_prompt_attend(q, kvb, p["wuv_t"], tq=t["attn_tq"], tk=t["attn_tk"], hb=t["attn_heads_per_block"])
    ys = [y.reshape(bsz * seq, BRANCH_DIM) for y in (y_conv, y_gla, y_mla)]
    x2 = _finish_layer(x2, z, ys, p, mix_tm=t["mix_tm"], ffn_tm=t["ffn_tm"])
    return x2, kv32, gla_new, conv_new


def _sample_layer(x2, bsz, n_tok, rope_tab, conv_buf, gla_s0, cache_t, page_table, p):
    t = SAMPLE_TILES
    m = bsz * n_tok
    z = norm_matmul(x2, p["norms"][0], p["w_in_t"], p["layer"], tm=m, tn=NZ_TILE)
    buf_rows = jnp.repeat(conv_buf.reshape(bsz, (CONV_WIDTH - 1) * CONV_DIM), n_tok, axis=0)
    y_conv, u = conv_sample(z, buf_rows, p["conv_w"], n_tok=n_tok, tc=t["conv_tc"])
    conv_new = u.reshape(bsz, n_tok, CONV_DIM)[:, n_tok - (CONV_WIDTH - 1):, :]
    y_gla, gla_new = gla_sample(z, gla_s0, p["layer"], p["wa2"], p["ba"], p["onorm"], n_tok=n_tok, tb=t["gla_tb"])
    q, kv32, kvb = mla_prep(z.reshape(1, m, NZ), rope_tab[0], rope_tab[1], p["q_norm"], p["kv_norm"],
                            p["wuq"], p["wuk_t"], tm=t["prep_tm"])
    qs = q.reshape(MLA_HEADS, bsz, n_tok, MLA_QK).transpose(1, 0, 2, 3).reshape(bsz, MLA_HEADS * n_tok, MLA_QK)
    kt_new = jnp.pad(jnp.swapaxes(kvb.reshape(bsz, n_tok, MLA_QK), 1, 2), ((0, 0), (0, 0), (0, PAGE_SIZE - n_tok)))
    o_lat = mla_sample_attend(qs, kt_new, cache_t, page_table, layer=p["layer"], n_tok=n_tok,
                              cp=t["attn_pages_per_chunk"])
    o_h = o_lat.reshape(bsz, MLA_HEADS, n_tok, MLA_KV_RANK).transpose(1, 0, 2, 3).reshape(MLA_HEADS, m, MLA_KV_RANK)
    y_mla = head_matmul(o_h, p["wuv_t"])
    x2 = _finish_layer(x2, z, [y_conv, y_gla, y_mla], p, mix_tm=m, ffn_tm=m)
    return x2, kv32.reshape(bsz, n_tok, MLA_QK), gla_new, conv_new


def kernel(x_prompt, x_sample, cache_mla, page_table, state_gla, state_conv, norms, w_in, conv_w, gla_wa2, gla_ba, gla_onorm, mla_q_norm, mla_kv_norm, mla_wuq, mla_wuk, mla_wuv, w_branch, w_out, ffn_w_gu, ffn_w_down):
    bp, sp, d = x_prompt.shape
    bs, ss, _ = x_sample.shape
    depth = norms.shape[0]
    past_len = page_table.shape[1] * PAGE_SIZE
    tab_p = _rope_tables(np.arange(sp), MLA_HEADS)
    tab_s = _rope_tables(past_len + np.arange(bs * ss) % ss, MLA_HEADS)
    xp = x_prompt.reshape(bp * sp, d)
    xs = x_sample.reshape(bs * ss, d)
    cache_t = jnp.swapaxes(cache_mla, 2, 3)
    big = _stacked_weights(w_in, w_branch, w_out, ffn_w_gu, ffn_w_down)
    outs = [[] for _ in range(6)]
    for l in range(depth):
        p = _layer_weights(l, big, norms, conv_w, gla_wa2, gla_ba, gla_onorm, mla_q_norm, mla_kv_norm, mla_wuq,
                           mla_wuk, mla_wuv)
        xp, kv_p, gla_p, conv_p = _prompt_layer(xp, bp, sp, tab_p, p)
        xs, kv_s, gla_s, conv_s = _sample_layer(xs, bs, ss, tab_s, state_conv[l], state_gla, cache_t, page_table, p)
        for acc, val in zip(outs, (kv_p, kv_s, gla_p, gla_s, conv_p, conv_s)):
            acc.append(val)
    return (xp.reshape(bp, sp, d), xs.reshape(bs, ss, d)) + tuple(jnp.stack(o) for o in outs)
```

```python
import functools
import math

import numpy as np
import jax
import jax.numpy as jnp
from jax import lax
from jax.experimental import pallas as pl
from jax.experimental.pallas import tpu as pltpu

F32 = jnp.float32
BF16 = jnp.bfloat16

D_MODEL = 2048
BRANCH_DIM = 1024
N_BRANCH = 3
CONV_DIM = BRANCH_DIM
CONV_WIDTH = 3
GLA_HEADS = 4
GLA_DK = 128
GLA_DV = BRANCH_DIM // GLA_HEADS
GLA_LOWRANK = 16
GLA_TAU = 16.0
GLA_CHUNK = 64
MLA_HEADS = 8
MLA_Q_RANK = 512
MLA_KV_RANK = 256
MLA_NOPE = 128
MLA_ROPE = 64
MLA_V = BRANCH_DIM // MLA_HEADS
MLA_SCALE = (MLA_NOPE + MLA_ROPE) ** -0.5
MLA_QK = MLA_KV_RANK + MLA_ROPE
ROPE_THETA = 10000.0
PAGE_SIZE = 128
D_FF = -(-8 * D_MODEL // (3 * 256)) * 256
RMS_EPS = 1e-6
NEG_INF = -1e30

LANES = 128
SUBLANES = 8
VMEM_LIMIT_BYTES = 56 * 1024 * 1024

Z_GATES = 0
Z_CONV = Z_GATES + N_BRANCH * D_MODEL
Z_GQ = Z_CONV + 3 * CONV_DIM
Z_GK = Z_GQ + GLA_HEADS * GLA_DK
Z_GV = Z_GK + GLA_HEADS * GLA_DK
Z_GG = Z_GV + GLA_HEADS * GLA_DV
Z_MQ = Z_GG + GLA_HEADS * GLA_DV
Z_MKV = Z_MQ + MLA_Q_RANK
Z_MISC = Z_MKV + MLA_KV_RANK
Z_END = Z_MISC + LANES
NZ_TILE = 512
NZ = -(-Z_END // NZ_TILE) * NZ_TILE
MISC_KR = 0
MISC_GA = MLA_ROPE


def _cparams(*sem):
    return pltpu.CompilerParams(dimension_semantics=sem, vmem_limit_bytes=VMEM_LIMIT_BYTES)


def _rms_rows(x, g):
    ms = jnp.mean(x * x, axis=-1, keepdims=True)
    return x * lax.rsqrt(ms + RMS_EPS) * g


def _norm_rows_to(dst_ref, src_ref, g_ref, rows, chunk=32):
    def body(c, carry):
        r0 = pl.multiple_of(c * chunk, chunk)
        dst_ref[pl.ds(r0, chunk), :] = _rms_rows(src_ref[pl.ds(r0, chunk), :], g_ref[...]).astype(dst_ref.dtype)
        return carry
    lax.fori_loop(0, rows // chunk, body, 0)


def _norm_mm_kernel(x_ref, g_ref, w_ref, o_ref, xn_ref):
    @pl.when(pl.program_id(1) == 0)
    def _():
        _norm_rows_to(xn_ref, x_ref, g_ref, x_ref.shape[0])
    o_ref[...] = _dot_nt(xn_ref[...], w_ref[...])


def norm_matmul(x, g, w_t, layer, *, tm, tn):
    m, k = x.shape
    n = w_t.shape[1]
    return pl.pallas_call(
        _norm_mm_kernel,
        grid=(m // tm, n // tn),
        in_specs=[pl.BlockSpec((tm, k), lambda i, j: (i, 0)),
                  pl.BlockSpec((1, k), lambda i, j: (0, 0)),
                  pl.BlockSpec((None, tn, k), lambda i, j: (layer, j, 0))],
        out_specs=pl.BlockSpec((tm, tn), lambda i, j: (i, j)),
        out_shape=jax.ShapeDtypeStruct((m, n), F32),
        scratch_shapes=[pltpu.VMEM((tm, k), BF16)],
        compiler_params=_cparams("parallel", "arbitrary"),
        name="in_proj",
    )(x, g.reshape(1, k), w_t)


def _conv_prompt_kernel(cb_ref, cc_ref, ch_ref, w_ref, y_ref, st_ref, carry_ref):
    i = pl.program_id(2)

    @pl.when(i == 0)
    def _():
        carry_ref[...] = jnp.zeros_like(carry_ref)

    u = cc_ref[...] * ch_ref[...]
    ts = u.shape[0]
    row = lax.broadcasted_iota(jnp.int32, u.shape, 0)
    prev1 = carry_ref[1:2, :]
    prev2 = carry_ref[0:1, :]
    um1 = jnp.where(row == 0, prev1, pltpu.roll(u, 1, axis=0))
    um2 = jnp.where(row == 0, prev2, jnp.where(row == 1, prev1, pltpu.roll(u, 2, axis=0)))
    w = w_ref[...]
    y = um2 * w[0:1, :] + um1 * w[1:2, :] + u * w[2:3, :]
    y_ref[...] = (cb_ref[...] * y).astype(y_ref.dtype)
    last = u[ts - 2:ts, :]
    carry_ref[...] = last

    @pl.when(i == pl.num_programs(2) - 1)
    def _():
        st_ref[...] = last


def conv_prompt(z3, conv_w, *, ts, tc):
    b, s, _ = z3.shape
    c = CONV_DIM
    base = Z_CONV // tc
    nc = c // tc

    def zspec(part):
        return pl.BlockSpec((None, ts, tc), lambda bi, ci, i, part=part: (bi, i, base + part * nc + ci))

    return pl.pallas_call(
        _conv_prompt_kernel,
        grid=(b, nc, s // ts),
        in_specs=[zspec(0), zspec(1), zspec(2),
                  pl.BlockSpec((CONV_WIDTH, tc), lambda bi, ci, i: (0, ci))],
        out_specs=[pl.BlockSpec((None, ts, tc), lambda bi, ci, i: (bi, i, ci)),
                   pl.BlockSpec((None, CONV_WIDTH - 1, tc), lambda bi, ci, i: (bi, 0, ci))],
        out_shape=[jax.ShapeDtypeStruct((b, s, c), BF16),
                   jax.ShapeDtypeStruct((b, CONV_WIDTH - 1, c), F32)],
        scratch_shapes=[pltpu.VMEM((CONV_WIDTH - 1, tc), F32)],
        compiler_params=_cparams("parallel", "parallel", "arbitrary"),
        name="conv_prompt",
    )(z3, z3, z3, conv_w)


def _conv_sample_kernel(cb_ref, cc_ref, ch_ref, b0_ref, b1_ref, w_ref, y_ref, u_ref, *, n_tok):
    u = cc_ref[...] * ch_ref[...]
    tpos = lax.broadcasted_iota(jnp.int32, u.shape, 0) % n_tok
    prev1 = b1_ref[...]
    um1 = jnp.where(tpos == 0, prev1, pltpu.roll(u, 1, axis=0))
    um2 = jnp.where(tpos == 0, b0_ref[...], jnp.where(tpos == 1, prev1, pltpu.roll(u, 2, axis=0)))
    w = w_ref[...]
    y = um2 * w[0:1, :] + um1 * w[1:2, :] + u * w[2:3, :]
    y_ref[...] = cb_ref[...] * y
    u_ref[...] = u


def conv_sample(z2, buf_rows, conv_w, *, n_tok, tc):
    m = z2.shape[0]
    c = CONV_DIM
    nc = c // tc
    base = Z_CONV // tc

    def zspec(part):
        return pl.BlockSpec((m, tc), lambda ci, part=part: (0, base + part * nc + ci))

    return pl.pallas_call(
        functools.partial(_conv_sample_kernel, n_tok=n_tok),
        grid=(nc,),
        in_specs=[zspec(0), zspec(1), zspec(2),
                  pl.BlockSpec((m, tc), lambda ci: (0, ci)),
                  pl.BlockSpec((m, tc), lambda ci: (0, nc + ci)),
                  pl.BlockSpec((CONV_WIDTH, tc), lambda ci: (0, ci))],
        out_specs=[pl.BlockSpec((m, tc), lambda ci: (0, ci)), pl.BlockSpec((m, tc), lambda ci: (0, ci))],
        out_shape=[jax.ShapeDtypeStruct((m, c), F32), jax.ShapeDtypeStruct((m, c), F32)],
        compiler_params=_cparams("parallel"),
        name="conv_sample",
    )(z2, z2, z2, buf_rows, buf_rows, conv_w)


def _log_sigmoid(x):
    return jnp.minimum(x, 0.0) - jnp.log1p(jnp.exp(-jnp.abs(x)))


def _split3(x):
    hi = x.astype(BF16)
    r1 = x - hi.astype(F32)
    mid = r1.astype(BF16)
    lo = (r1 - mid.astype(F32)).astype(BF16)
    return hi, mid, lo


def _col_bcast(row, width):
    n = row.shape[1]
    sq = jnp.transpose(jnp.broadcast_to(row, (n, n)))
    return jnp.concatenate([sq] * (width // n), axis=1)


def _dot_nt(a, b):
    return lax.dot_general(a, b, (((1,), (1,)), ((), ())), preferred_element_type=F32)


def _dot_tn(a, b):
    return lax.dot_general(a, b, (((0,), (0,)), ((), ())), preferred_element_type=F32)


def _gla_gate_norm(o, g, onorm):
    on = o * lax.rsqrt(jnp.mean(o * o, axis=-1, keepdims=True) + RMS_EPS) * onorm
    return on * (g * jax.nn.sigmoid(g))


def _pad_rows(x, rows):
    if x.shape[0] == rows:
        return x
    return jnp.concatenate([x, jnp.zeros((rows - x.shape[0], x.shape[1]), x.dtype)], axis=0)


def _gla_log_decay(misc, wa2_ref, ba_ref):
    lane = lax.broadcasted_iota(jnp.int32, misc.shape, 1)
    a_low = jnp.where((lane >= MISC_GA) & (lane < MISC_GA + GLA_LOWRANK), misc, 0.0).astype(BF16)
    a = jnp.dot(a_low, wa2_ref[...], preferred_element_type=F32) + ba_ref[...]
    return _log_sigmoid(a) * (1.0 / GLA_TAU)


def _gla_prompt_kernel(q_ref, k_ref, v_ref, g_ref, a_ref, wa2_ref, ba_ref, on_ref, y_ref, st_ref, s_ref, *, ts):
    i = pl.program_id(1)

    @pl.when(i == 0)
    def _():
        s_ref[...] = jnp.zeros_like(s_ref)

    c = GLA_CHUNK
    pc = LANES
    rr = lax.broadcasted_iota(jnp.int32, (c, pc), 0)
    cc = lax.broadcasted_iota(jnp.int32, (c, pc), 1)
    tril = rr >= cc
    tril_b = tril.astype(BF16)
    scale = GLA_DK ** -0.5
    for ci in range(ts // c):
        r = slice(ci * c, (ci + 1) * c)
        la = _gla_log_decay(a_ref[r, :], wa2_ref, ba_ref)
        big_l = None
        for term in _split3(_pad_rows(la, pc)):
            part = jnp.dot(tril_b, term, preferred_element_type=F32)
            big_l = part if big_l is None else big_l + part
        l_last = big_l[c - 1:c, :]
        e_pos = jnp.exp(big_l)
        e_neg = jnp.exp(-big_l)
        e_dec = jnp.exp(l_last - big_l)
        e_last = jnp.exp(l_last)
        heads = range(GLA_HEADS)
        hs = [slice(h * GLA_DK, (h + 1) * GLA_DK) for h in heads]
        vs = [slice(h * GLA_DV, (h + 1) * GLA_DV) for h in heads]
        ks = [k_ref[r, hs[h]] for h in heads]
        q_t = [(q_ref[r, hs[h]] * scale * e_pos[:, hs[h]]).astype(BF16) for h in heads]
        attn_raw = [_dot_nt(q_t[h], _pad_rows(ks[h] * e_neg[:, hs[h]], pc).astype(BF16)) for h in heads]
        v = [_pad_rows(v_ref[r, vs[h]], pc).astype(BF16) for h in heads]
        s_old = [s_ref[h] for h in heads]
        o_inter = [jnp.dot(q_t[h], s_old[h].astype(BF16), preferred_element_type=F32) for h in heads]
        s_add = [_dot_tn(_pad_rows(ks[h] * e_dec[:, hs[h]], pc).astype(BF16), v[h]) for h in heads]
        for h in heads:
            attn = jnp.where(tril, attn_raw[h], 0.0).astype(BF16)
            o = o_inter[h] + jnp.dot(attn, v[h], preferred_element_type=F32)
            s_ref[h] = s_old[h] * _col_bcast(e_last[:, hs[h]], GLA_DV) + s_add[h]
            y_ref[r, vs[h]] = _gla_gate_norm(o, g_ref[r, vs[h]], on_ref[...]).astype(y_ref.dtype)

    @pl.when(i == pl.num_programs(1) - 1)
    def _():
        st_ref[...] = s_ref[...]


def gla_prompt(z3, wa2, ba, onorm, *, ts):
    b, s, _ = z3.shape
    hk = GLA_HEADS * GLA_DK
    hv = GLA_HEADS * GLA_DV
    return pl.pallas_call(
        functools.partial(_gla_prompt_kernel, ts=ts),
        grid=(b, s // ts),
        in_specs=[pl.BlockSpec((None, ts, hk), lambda bi, i: (bi, i, Z_GQ // hk)),
                  pl.BlockSpec((None, ts, hk), lambda bi, i: (bi, i, Z_GK // hk)),
                  pl.BlockSpec((None, ts, hv), lambda bi, i: (bi, i, Z_GV // hv)),
                  pl.BlockSpec((None, ts, hv), lambda bi, i: (bi, i, Z_GG // hv)),
                  pl.BlockSpec((None, ts, LANES), lambda bi, i: (bi, i, Z_MISC // LANES)),
                  pl.BlockSpec((LANES, hk), lambda bi, i: (0, 0)),
                  pl.BlockSpec((1, hk), lambda bi, i: (0, 0)),
                  pl.BlockSpec((1, GLA_DV), lambda bi, i: (0, 0))],
        out_specs=[pl.BlockSpec((None, ts, hv), lambda bi, i: (bi, i, 0)),
                   pl.BlockSpec((None, GLA_HEADS, GLA_DK, GLA_DV), lambda bi, i: (bi, 0, 0, 0))],
        out_shape=[jax.ShapeDtypeStruct((b, s, hv), BF16),
                   jax.ShapeDtypeStruct((b, GLA_HEADS, GLA_DK, GLA_DV), F32)],
        scratch_shapes=[pltpu.VMEM((GLA_HEADS, GLA_DK, GLA_DV), F32)],
        compiler_params=_cparams("parallel", "arbitrary"),
        name="gla_prompt",
    )(z3, z3, z3, z3, z3, wa2, ba.reshape(1, hk), onorm.reshape(1, GLA_DV))


def _gla_sample_kernel(q_ref, k_ref, v_ref, g_ref, a_ref, wa2_ref, ba_ref, on_ref, s0_ref, y_ref, st_ref,
                       *, n_tok, n_groups):
    rows = SUBLANES
    per = rows // n_tok
    pr = 2 * SUBLANES
    pc = LANES
    rr = lax.broadcasted_iota(jnp.int32, (pr, pc), 0)
    cc = lax.broadcasted_iota(jnp.int32, (pr, pc), 1)
    causal = (rr // n_tok == cc // n_tok) & (rr >= cc)
    trow = lax.broadcasted_iota(jnp.int32, (rows, 1), 0)
    tpos = trow % n_tok
    scale = GLA_DK ** -0.5

    def body(p, carry):
        r0 = pl.multiple_of(p * rows, rows)
        r = pl.ds(r0, rows)
        la = _gla_log_decay(a_ref[r, :], wa2_ref, ba_ref)
        big_l = la
        for d in range(1, n_tok):
            big_l = big_l + jnp.where(tpos >= d, pltpu.roll(la, d, axis=0), 0.0)
        l_last = jnp.zeros_like(big_l)
        for j in range(per):
            last_j = big_l[(j + 1) * n_tok - 1:(j + 1) * n_tok, :]
            l_last = jnp.where(trow // n_tok == j, last_j, l_last)
        e_pos = jnp.exp(big_l)
        e_neg = jnp.exp(-big_l)
        e_dec = jnp.exp(l_last - big_l)
        for h in range(GLA_HEADS):
            hs = slice(h * GLA_DK, (h + 1) * GLA_DK)
            vs = slice(h * GLA_DV, (h + 1) * GLA_DV)
            k = k_ref[r, hs]
            q_t = _pad_rows(q_ref[r, hs] * scale * e_pos[:, hs], pr).astype(BF16)
            k_t = _pad_rows(k * e_neg[:, hs], pc).astype(BF16)
            k_dec = k * e_dec[:, hs]
            v = _pad_rows(v_ref[r, vs], pc).astype(BF16)
            attn = jnp.where(causal, _dot_nt(q_t, k_t), 0.0).astype(BF16)
            o = jnp.dot(attn, v, preferred_element_type=F32)[0:rows]
            for j in range(per):
                seq = p * per + j
                in_seq = trow // n_tok == j
                s_old = s0_ref[seq, h]
                o_inter = jnp.dot(q_t, s_old.astype(BF16), preferred_element_type=F32)[0:rows]
                o = o + jnp.where(in_seq, o_inter, 0.0)
                e_last = jnp.exp(big_l[(j + 1) * n_tok - 1:(j + 1) * n_tok, hs])
                k_dec_j = _pad_rows(jnp.where(in_seq, k_dec, 0.0), pc).astype(BF16)
                st_ref[seq, h] = s_old * _col_bcast(e_last, GLA_DV) + _dot_tn(k_dec_j, v)
            y_ref[r, vs] = _gla_gate_norm(o, g_ref[r, vs], on_ref[...])
        return carry

    lax.fori_loop(0, n_groups, body, 0)


def gla_sample(z2, s0_all, layer, wa2, ba, onorm, *, n_tok, tb):
    m = z2.shape[0]
    b = m // n_tok
    hk = GLA_HEADS * GLA_DK
    hv = GLA_HEADS * GLA_DV
    tr = tb * n_tok
    st_spec = pl.BlockSpec((tb, GLA_HEADS, GLA_DK, GLA_DV), lambda i: (i, 0, 0, 0))
    s0_spec = pl.BlockSpec((None, tb, GLA_HEADS, GLA_DK, GLA_DV), lambda i: (layer, i, 0, 0, 0))
    return pl.pallas_call(
        functools.partial(_gla_sample_kernel, n_tok=n_tok, n_groups=tr // SUBLANES),
        grid=(b // tb,),
        in_specs=[pl.BlockSpec((tr, hk), lambda i: (i, Z_GQ // hk)),
                  pl.BlockSpec((tr, hk), lambda i: (i, Z_GK // hk)),
                  pl.BlockSpec((tr, hv), lambda i: (i, Z_GV // hv)),
                  pl.BlockSpec((tr, hv), lambda i: (i, Z_GG // hv)),
                  pl.BlockSpec((tr, LANES), lambda i: (i, Z_MISC // LANES)),
                  pl.BlockSpec((LANES, hk), lambda i: (0, 0)),
                  pl.BlockSpec((1, hk), lambda i: (0, 0)),
                  pl.BlockSpec((1, GLA_DV), lambda i: (0, 0)),
                  s0_spec],
        out_specs=[pl.BlockSpec((tr, hv), lambda i: (i, 0)), st_spec],
        out_shape=[jax.ShapeDtypeStruct((m, hv), F32),
                   jax.ShapeDtypeStruct(s0_all.shape[1:], F32)],
        compiler_params=_cparams("parallel"),
        name="gla_sample",
    )(z2, z2, z2, z2, z2, wa2, ba.reshape(1, hk), onorm.reshape(1, GLA_DV), s0_all)


def _rope_tables(pos, reps):
    half = MLA_ROPE // 2
    inv_freq = ROPE_THETA ** (-np.arange(half, dtype=np.float64) / half)
    ang = np.asarray(pos, np.float64)[:, None] * inv_freq[None, :]
    cos = np.concatenate([np.cos(ang), np.cos(ang)], axis=1)
    sin = np.concatenate([-np.sin(ang), np.sin(ang)], axis=1)
    return (jnp.asarray(np.tile(cos, (1, reps)), F32), jnp.asarray(np.tile(sin, (1, reps)), F32))


def _rope(x, cos, sin):
    n = x.shape[1]
    half = MLA_ROPE // 2
    lane = lax.broadcasted_iota(jnp.int32, x.shape, 1)
    partner = jnp.where(lane % MLA_ROPE < half, pltpu.roll(x, n - half, axis=1), pltpu.roll(x, half, axis=1))
    return x * cos + partner * sin


def _mla_prep_kernel(mq_ref, mkv_ref, misc_ref, cos_ref, sin_ref, qn_ref, kvn_ref, wuq_ref, wuk_ref,
                     q_out, kv32_out, kvb_out, cq_ref, q_ref):
    tm = mq_ref.shape[0]
    _norm_rows_to(cq_ref, mq_ref, qn_ref, tm)
    q_ref[...] = jnp.dot(cq_ref[...], wuq_ref[...], preferred_element_type=F32)
    n_nope = MLA_HEADS * MLA_NOPE
    q_rope = _rope(q_ref[:, n_nope:], cos_ref[...], sin_ref[...])
    for h in range(MLA_HEADS):
        q_nope = q_ref[:, h * MLA_NOPE:(h + 1) * MLA_NOPE].astype(BF16)
        q_out[h, :, 0:MLA_KV_RANK] = jnp.dot(q_nope, wuk_ref[h], preferred_element_type=F32).astype(BF16)
        q_out[h, :, MLA_KV_RANK:MLA_QK] = q_rope[:, h * MLA_ROPE:(h + 1) * MLA_ROPE].astype(BF16)
    c = _rms_rows(mkv_ref[...], kvn_ref[...])
    kr = _rope(misc_ref[...], cos_ref[:, 0:LANES], sin_ref[:, 0:LANES])[:, MISC_KR:MISC_KR + MLA_ROPE]
    kv32_out[:, 0:MLA_KV_RANK] = c
    kv32_out[:, MLA_KV_RANK:MLA_QK] = kr
    kvb_out[:, 0:MLA_KV_RANK] = c.astype(BF16)
    kvb_out[:, MLA_KV_RANK:MLA_QK] = kr.astype(BF16)


def mla_prep(z3, cos, sin, q_norm, kv_norm, wuq, wuk_t, *, tm):
    b, s, _ = z3.shape
    nq = MLA_HEADS * (MLA_NOPE + MLA_ROPE)
    nr = MLA_HEADS * MLA_ROPE
    const2 = lambda bi, i: (0, 0)
    return pl.pallas_call(
        _mla_prep_kernel,
        grid=(b, s // tm),
        in_specs=[pl.BlockSpec((None, tm, MLA_Q_RANK), lambda bi, i: (bi, i, Z_MQ // MLA_Q_RANK)),
                  pl.BlockSpec((None, tm, MLA_KV_RANK), lambda bi, i: (bi, i, Z_MKV // MLA_KV_RANK)),
                  pl.BlockSpec((None, tm, LANES), lambda bi, i: (bi, i, Z_MISC // LANES)),
                  pl.BlockSpec((tm, nr), lambda bi, i: (i, 0)),
                  pl.BlockSpec((tm, nr), lambda bi, i: (i, 0)),
                  pl.BlockSpec((1, MLA_Q_RANK), const2),
                  pl.BlockSpec((1, MLA_KV_RANK), const2),
                  pl.BlockSpec((MLA_Q_RANK, nq), const2),
                  pl.BlockSpec((MLA_HEADS, MLA_NOPE, MLA_KV_RANK), lambda bi, i: (0, 0, 0))],
        out_specs=[pl.BlockSpec((None, MLA_HEADS, tm, MLA_QK), lambda bi, i: (bi, 0, i, 0)),
                   pl.BlockSpec((None, tm, MLA_QK), lambda bi, i: (bi, i, 0)),
                   pl.BlockSpec((None, tm, MLA_QK), lambda bi, i: (bi, i, 0))],
        out_shape=[jax.ShapeDtypeStruct((b, MLA_HEADS, s, MLA_QK), BF16),
                   jax.ShapeDtypeStruct((b, s, MLA_QK), F32),
                   jax.ShapeDtypeStruct((b, s, MLA_QK), BF16)],
        scratch_shapes=[pltpu.VMEM((tm, MLA_Q_RANK), BF16), pltpu.VMEM((tm, nq), F32)],
        compiler_params=_cparams("parallel", "parallel"),
        name="mla_prep",
    )(z3, z3, z3, cos, sin, q_norm.reshape(1, -1), kv_norm.reshape(1, -1), wuq, wuk_t)


SOFTMAX_EXP2_SCALE = MLA_SCALE * math.log2(math.e)


def _online_softmax(s, kv_c_t, m_prev, l_prev, acc_prev, contract_kv_lanes):
    m_new = jnp.maximum(m_prev, jnp.max(s, axis=-1, keepdims=True))
    corr = jnp.exp2((m_prev - m_new) * SOFTMAX_EXP2_SCALE)
    p = jnp.exp2((s - m_new) * SOFTMAX_EXP2_SCALE)
    pv = _dot_nt(p.astype(BF16), kv_c_t) if contract_kv_lanes else jnp.dot(
        p.astype(BF16), kv_c_t, preferred_element_type=F32)
    return m_new, l_prev * corr + jnp.sum(p, axis=-1, keepdims=True), acc_prev * corr + pv


def _mla_prompt_kernel(q_ref, kv_ref, wuv_ref, y_ref, m_ref, l_ref, acc_ref,
                       s0_ref, s1_ref, p0_ref, p1_ref, c0_ref, c1_ref, *, tq, tk, hb):
    i = pl.program_id(1)
    rb = hb * tq
    n_blocks = MLA_HEADS // hb
    shift = n_blocks.bit_length() - 1
    assert n_blocks == 1 << shift
    m_ref[...] = jnp.full_like(m_ref, NEG_INF)
    l_ref[...] = jnp.zeros_like(l_ref)
    acc_ref[...] = jnp.zeros_like(acc_ref)
    n_items = ((i * tq + tq - 1) // tk + 1) * n_blocks

    def key_rows(t):
        return pl.ds(pl.multiple_of(lax.shift_right_logical(t, shift) * tk, tk), tk)

    def scores(t, s_ref):
        r = t & (n_blocks - 1)
        q = q_ref[pl.ds(r * hb, hb)].reshape(rb, MLA_QK)
        s = _dot_nt(q, kv_ref[key_rows(t), :])
        qpos = i * tq + lax.broadcasted_iota(jnp.int32, s.shape, 0) % tq
        kpos = lax.shift_right_logical(t, shift) * tk + lax.broadcasted_iota(jnp.int32, s.shape, 1)
        s_ref[...] = jnp.where(kpos <= qpos, s, NEG_INF)

    def softmax_stats(t, s_ref, p_ref, c_ref):
        r = t & (n_blocks - 1)
        s = s_ref[...]
        m_prev = m_ref[r]
        m_new = jnp.maximum(m_prev, jnp.max(s, axis=-1, keepdims=True))
        corr = jnp.exp2((m_prev - m_new) * SOFTMAX_EXP2_SCALE)
        p = jnp.exp2((s - m_new) * SOFTMAX_EXP2_SCALE)
        l_ref[r] = l_ref[r] * corr + jnp.sum(p, axis=-1, keepdims=True)
        m_ref[r] = m_new
        p_ref[...] = p.astype(BF16)
        c_ref[...] = corr

    def accumulate(t, p_ref, c_ref):
        r = t & (n_blocks - 1)
        kv_c = kv_ref[key_rows(t), 0:MLA_KV_RANK]
        acc_ref[r] = acc_ref[r] * c_ref[...] + jnp.dot(p_ref[...], kv_c, preferred_element_type=F32)

    assert n_blocks % 2 == 0 and n_blocks >= 4
    buf0, buf1 = (s0_ref, p0_ref, c0_ref), (s1_ref, p1_ref, c1_ref)

    def step(t, cur, nxt, with_scores=True):
        if with_scores:
            scores(t + 1, nxt[0])
        softmax_stats(t, *cur)
        accumulate(t - 1, nxt[1], nxt[2])

    def step_pair(u, carry):
        t = 2 * u + 1
        step(t, buf1, buf0)
        step(t + 1, buf0, buf1)
        return carry

    scores(jnp.int32(0), s0_ref)
    scores(jnp.int32(1), s1_ref)
    softmax_stats(jnp.int32(0), *buf0)
    lax.fori_loop(0, lax.shift_right_logical(n_items, 1) - 1, step_pair, 0)
    step(n_items - 1, buf1, buf0, with_scores=False)
    accumulate(n_items - 1, p1_ref, c1_ref)

    for h in range(MLA_HEADS):
        r, hs = h // hb, slice((h % hb) * tq, (h % hb + 1) * tq)
        o = (acc_ref[r, hs, :] / l_ref[r, hs, :]).astype(BF16)
        y_ref[:, h * MLA_V:(h + 1) * MLA_V] = jnp.dot(o, wuv_ref[h], preferred_element_type=F32).astype(y_ref.dtype)


def mla_prompt_attend(q, kvb, wuv_t, *, tq, tk, hb):
    b, _, s, _ = q.shape
    rb = hb * tq
    n_blocks = MLA_HEADS // hb
    return pl.pallas_call(
        functools.partial(_mla_prompt_kernel, tq=tq, tk=tk, hb=hb),
        grid=(b, s // tq),
        in_specs=[pl.BlockSpec((None, MLA_HEADS, tq, MLA_QK), lambda bi, i: (bi, 0, i, 0)),
                  pl.BlockSpec((None, s, MLA_QK), lambda bi, i: (bi, 0, 0)),
                  pl.BlockSpec((MLA_HEADS, MLA_KV_RANK, MLA_V), lambda bi, i: (0, 0, 0))],
        out_specs=pl.BlockSpec((None, tq, BRANCH_DIM), lambda bi, i: (bi, i, 0)),
        out_shape=jax.ShapeDtypeStruct((b, s, BRANCH_DIM), BF16),
        scratch_shapes=[pltpu.VMEM((n_blocks, rb, 1), F32), pltpu.VMEM((n_blocks, rb, 1), F32),
                        pltpu.VMEM((n_blocks, rb, MLA_KV_RANK), F32),
                        pltpu.VMEM((rb, tk), F32), pltpu.VMEM((rb, tk), F32),
                        pltpu.VMEM((rb, tk), BF16), pltpu.VMEM((rb, tk), BF16),
                        pltpu.VMEM((rb, 1), F32), pltpu.VMEM((rb, 1), F32)],
        compiler_params=_cparams("parallel", "arbitrary"),
        name="mla_prompt_attn",
    )(q, kvb, wuv_t)


def _lane_row(col):
    rows = col.shape[0]
    sq = jnp.transpose(jnp.broadcast_to(_pad_rows(col, LANES), (LANES, LANES)))
    return sq[0:1, 0:rows]


def _absorb_keys(s, kt_c, carry):
    m_prev, l_prev, acc_t = carry
    m_new = jnp.maximum(m_prev, jnp.max(s, axis=-1, keepdims=True))
    corr = jnp.exp2((m_prev - m_new) * SOFTMAX_EXP2_SCALE)
    p = jnp.exp2((s - m_new) * SOFTMAX_EXP2_SCALE)
    l_new = l_prev * corr + jnp.sum(p, axis=-1, keepdims=True)
    return m_new, l_new, acc_t * _lane_row(corr) + _dot_nt(kt_c, p.astype(BF16))


def _mla_sample_kernel(pt_ref, q_ref, new_ref, cache_ref, o_ref, f0, f1, h0, h1, s0, s1, sem,
                       *, layer, n_tok, n_chunks, cp):
    b = pl.program_id(0)
    nb = pl.num_programs(0)
    fbuf, hbuf, sbuf = (f0, f1), (h0, h1), (s0, s1)

    def page_copy(bi, c, p, slot):
        page = pt_ref[bi, c * cp + p]
        return pltpu.make_async_copy(cache_ref.at[layer, page], fbuf[slot].at[p], sem.at[slot])

    def start_chunk(bi, c, slot):
        for p in range(cp):
            page_copy(bi, c, p, slot).start()

    def start_ahead(c, slot):
        @pl.when(c < n_chunks)
        def _():
            start_chunk(b, c, slot)

        @pl.when((c >= n_chunks) & (b + 1 < nb))
        def _():
            start_chunk(b + 1, c - n_chunks, slot)

    def score_chunk(c, slot):
        for p in range(cp):
            page_copy(b, c, p, slot).wait()
        for p in range(cp):
            hbuf[slot][:, p * PAGE_SIZE:(p + 1) * PAGE_SIZE] = fbuf[slot][p].astype(BF16)
        sbuf[slot][...] = jnp.dot(q, hbuf[slot][...], preferred_element_type=F32)

    def absorb_chunk(slot, carry):
        return _absorb_keys(sbuf[slot][...], hbuf[slot][0:MLA_KV_RANK, :], carry)

    @pl.when(b == 0)
    def _():
        start_chunk(b, 0, 0)
        start_chunk(b, 1, 1)

    q = q_ref[...]
    rows = q.shape[0]
    score_chunk(0, 0)

    def chunk_pair(c2, carry):
        c = 2 * c2
        start_ahead(c + 2, 0)
        score_chunk(c + 1, 1)
        carry = absorb_chunk(0, carry)
        start_ahead(c + 3, 1)
        score_chunk(c + 2, 0)
        return absorb_chunk(1, carry)

    carry = (jnp.full((rows, 1), NEG_INF, F32), jnp.zeros((rows, 1), F32), jnp.zeros((MLA_KV_RANK, rows), F32))
    carry = lax.fori_loop(0, n_chunks // 2 - 1, chunk_pair, carry)
    start_ahead(n_chunks, 0)
    score_chunk(n_chunks - 1, 1)
    carry = absorb_chunk(0, carry)
    start_ahead(n_chunks + 1, 1)
    carry = absorb_chunk(1, carry)

    kt_new = new_ref[...]
    s = jnp.dot(q, kt_new, preferred_element_type=F32)
    tpos = lax.broadcasted_iota(jnp.int32, s.shape, 0) % n_tok
    kpos = lax.broadcasted_iota(jnp.int32, s.shape, 1)
    m, l, acc_t = _absorb_keys(jnp.where(kpos <= tpos, s, NEG_INF), kt_new[0:MLA_KV_RANK, :], carry)
    o_t = acc_t / _lane_row(l)
    o_sq = jnp.transpose(jnp.concatenate([o_t, jnp.zeros((MLA_KV_RANK, LANES - rows), F32)], axis=1))
    o_ref[...] = o_sq[0:rows, :].astype(o_ref.dtype)


def mla_sample_attend(q, kt_new_pad, cache_t, page_table, *, layer, n_tok, cp):
    b, rows, _ = q.shape
    n_pages = page_table.shape[1]
    n_chunks = n_pages // cp
    keys = cp * PAGE_SIZE
    assert n_chunks % 2 == 0 and n_chunks * cp == n_pages and rows <= LANES
    grid_spec = pltpu.PrefetchScalarGridSpec(
        num_scalar_prefetch=1,
        grid=(b,),
        in_specs=[pl.BlockSpec((None, rows, MLA_QK), lambda bi, pt: (bi, 0, 0)),
                  pl.BlockSpec((None, MLA_QK, PAGE_SIZE), lambda bi, pt: (bi, 0, 0)),
                  pl.BlockSpec(memory_space=pl.ANY)],
        out_specs=pl.BlockSpec((None, rows, MLA_KV_RANK), lambda bi, pt: (bi, 0, 0)),
        scratch_shapes=[pltpu.VMEM((cp, MLA_QK, PAGE_SIZE), F32), pltpu.VMEM((cp, MLA_QK, PAGE_SIZE), F32),
                        pltpu.VMEM((MLA_QK, keys), BF16), pltpu.VMEM((MLA_QK, keys), BF16),
                        pltpu.VMEM((rows, keys), F32), pltpu.VMEM((rows, keys), F32),
                        pltpu.SemaphoreType.DMA((2,))],
    )
    return pl.pallas_call(
        functools.partial(_mla_sample_kernel, layer=layer, n_tok=n_tok, n_chunks=n_chunks, cp=cp),
        grid_spec=grid_spec,
        out_shape=jax.ShapeDtypeStruct((b, rows, MLA_KV_RANK), BF16),
        compiler_params=_cparams("arbitrary"),
        name="mla_sample_attn",
    )(page_table, q, kt_new_pad, cache_t)


def _head_mm_kernel(o_ref, w_ref, y_ref):
    y_ref[...] = jnp.dot(o_ref[...], w_ref[...], preferred_element_type=F32).astype(y_ref.dtype)


def head_matmul(o, w):
    h, m, k = o.shape
    n = w.shape[2]
    return pl.pallas_call(
        _head_mm_kernel,
        grid=(h,),
        in_specs=[pl.BlockSpec((None, m, k), lambda i: (i, 0, 0)),
                  pl.BlockSpec((None, k, n), lambda i: (i, 0, 0))],
        out_specs=pl.BlockSpec((m, n), lambda i: (0, i)),
        out_shape=jax.ShapeDtypeStruct((m, h * n), BF16),
        compiler_params=_cparams("parallel"),
        name="mla_value_up",
    )(o, w)


def _branch_mix_kernel(y0_ref, y1_ref, y2_ref, g0_ref, g1_ref, g2_ref, w_ref, o_ref):
    mix = None
    for n, (y_ref, g_ref) in enumerate(((y0_ref, g0_ref), (y1_ref, g1_ref), (y2_ref, g2_ref))):
        proj = jnp.dot(y_ref[...].astype(BF16), w_ref[n], preferred_element_type=F32)
        term = jax.nn.sigmoid(g_ref[...]) * proj
        mix = term if mix is None else mix + term
    o_ref[...] = mix.astype(o_ref.dtype)


def branch_mix(ys, z, w_branch, layer, *, tm, tn):
    m = z.shape[0]
    per = D_MODEL // tn
    yspec = pl.BlockSpec((tm, BRANCH_DIM), lambda i, j: (i, 0))

    def gspec(n):
        return pl.BlockSpec((tm, tn), lambda i, j, n=n: (i, Z_GATES // tn + n * per + j))

    return pl.pallas_call(
        _branch_mix_kernel,
        grid=(m // tm, per),
        in_specs=[yspec, yspec, yspec, gspec(0), gspec(1), gspec(2),
                  pl.BlockSpec((None, N_BRANCH, BRANCH_DIM, tn), lambda i, j: (layer, 0, 0, j))],
        out_specs=pl.BlockSpec((tm, tn), lambda i, j: (i, j)),
        out_shape=jax.ShapeDtypeStruct((m, D_MODEL), BF16),
        compiler_params=_cparams("parallel", "arbitrary"),
        name="branch_mix",
    )(*ys, z, z, z, w_branch)


def _add_norm_rows(o_ref, x_ref, f_ref, g_ref, rows, chunk=32):
    def body(c, carry):
        r = pl.ds(pl.multiple_of(c * chunk, chunk), chunk)
        o_ref[r, :] = x_ref[r, :] + _rms_rows(f_ref[r, :], g_ref[...])
        return carry
    lax.fori_loop(0, rows // chunk, body, 0)


def _out_proj_kernel(mix_ref, w_ref, x_ref, g_ref, o_ref, f_ref):
    f_ref[...] = jnp.dot(mix_ref[...], w_ref[...], preferred_element_type=F32)
    _add_norm_rows(o_ref, x_ref, f_ref, g_ref, x_ref.shape[0])


def out_proj_residual(mix, w_out, layer, x, g, *, tm):
    m, d = x.shape
    return pl.pallas_call(
        _out_proj_kernel,
        grid=(m // tm,),
        in_specs=[pl.BlockSpec((tm, d), lambda i: (i, 0)),
                  pl.BlockSpec((None, d, d), lambda i: (layer, 0, 0)),
                  pl.BlockSpec((tm, d), lambda i: (i, 0)),
                  pl.BlockSpec((1, d), lambda i: (0, 0))],
        out_specs=pl.BlockSpec((tm, d), lambda i: (i, 0)),
        out_shape=jax.ShapeDtypeStruct((m, d), F32),
        scratch_shapes=[pltpu.VMEM((tm, d), F32)],
        compiler_params=_cparams("parallel"),
        name="out_proj",
    )(mix, w_out, x, g.reshape(1, d))


def _ffn_kernel(x_ref, g_in_ref, wg_ref, wu_ref, wd_ref, g_out_ref, o_ref, xn_ref, acc_ref):
    j = pl.program_id(1)

    @pl.when(j == 0)
    def _():
        _norm_rows_to(xn_ref, x_ref, g_in_ref, x_ref.shape[0])
        acc_ref[...] = jnp.zeros_like(acc_ref)

    xn = xn_ref[...]
    gate = jnp.dot(xn, wg_ref[...], preferred_element_type=F32)
    up = jnp.dot(xn, wu_ref[...], preferred_element_type=F32)
    act = (gate * jax.nn.sigmoid(gate) * up).astype(BF16)
    acc_ref[...] += jnp.dot(act, wd_ref[...], preferred_element_type=F32)

    @pl.when(j == pl.num_programs(1) - 1)
    def _():
        _add_norm_rows(o_ref, x_ref, acc_ref, g_out_ref, x_ref.shape[0])


def ffn(x, g_in, w_gu, w_down, layer, g_out, *, tm, tf):
    m, d = x.shape
    nf = D_FF // tf
    return pl.pallas_call(
        _ffn_kernel,
        grid=(m // tm, nf),
        in_specs=[pl.BlockSpec((tm, d), lambda i, j: (i, 0)),
                  pl.BlockSpec((1, d), lambda i, j: (0, 0)),
                  pl.BlockSpec((None, d, tf), lambda i, j: (layer, 0, j)),
                  pl.BlockSpec((None, d, tf), lambda i, j: (layer, 0, j + nf)),
                  pl.BlockSpec((None, tf, d), lambda i, j: (layer, j, 0)),
                  pl.BlockSpec((1, d), lambda i, j: (0, 0))],
        out_specs=pl.BlockSpec((tm, d), lambda i, j: (i, 0)),
        out_shape=jax.ShapeDtypeStruct((m, d), F32),
        scratch_shapes=[pltpu.VMEM((tm, d), BF16), pltpu.VMEM((tm, d), F32)],
        compiler_params=_cparams("parallel", "arbitrary"),
        name="ffn",
    )(x, g_in.reshape(1, d), w_gu, w_gu, w_down, g_out.reshape(1, d))


def _prep_w_in_t(w_in):
    splits = (CONV_DIM, CONV_DIM, CONV_DIM,
              GLA_HEADS * GLA_DK, GLA_HEADS * GLA_DK, GLA_HEADS * GLA_DV, GLA_LOWRANK, GLA_HEADS * GLA_DV,
              MLA_Q_RANK, MLA_KV_RANK, MLA_ROPE, N_BRANCH * D_MODEL)
    idx = np.cumsum(splits)[:-1].tolist()
    w_t = jnp.swapaxes(w_in, 1, 2)
    cb, cc, ch, gq, gk, gv, ga, gg, mq, mkv, mkr, gates = jnp.split(w_t, idx, axis=1)
    depth, _, d = w_t.shape
    zeros = jnp.zeros((depth, LANES - MLA_ROPE - GLA_LOWRANK + NZ - Z_END, d), w_t.dtype)
    return jnp.concatenate([gates, cb, cc, ch, gq, gk, gv, gg, mq, mkv, mkr, ga, zeros], axis=1).astype(BF16)


def _prep_wuq(wuq):
    w = wuq.reshape(MLA_Q_RANK, MLA_HEADS, MLA_NOPE + MLA_ROPE)
    nope = w[:, :, :MLA_NOPE].reshape(MLA_Q_RANK, MLA_HEADS * MLA_NOPE)
    rope = w[:, :, MLA_NOPE:].reshape(MLA_Q_RANK, MLA_HEADS * MLA_ROPE)
    return jnp.concatenate([nope, rope], axis=1).astype(BF16)


def _stacked_weights(w_in, w_branch, w_out, ffn_w_gu, ffn_w_down):
    return dict(w_in_t=_prep_w_in_t(w_in), w_branch=w_branch.astype(BF16), w_out=w_out.astype(BF16),
                w_gu=ffn_w_gu.astype(BF16), w_down=ffn_w_down.astype(BF16))


def _layer_weights(l, big, norms, conv_w, gla_wa2, gla_ba, gla_onorm, mla_q_norm, mla_kv_norm, mla_wuq,
                   mla_wuk, mla_wuv):
    return dict(
        big, layer=l, norms=norms[l], conv_w=conv_w[l],
        wa2=jnp.pad(gla_wa2[l], ((MISC_GA, LANES - MISC_GA - GLA_LOWRANK), (0, 0))).astype(BF16),
        ba=gla_ba[l], onorm=gla_onorm[l],
        q_norm=mla_q_norm[l], kv_norm=mla_kv_norm[l], wuq=_prep_wuq(mla_wuq[l]),
        wuk_t=jnp.transpose(mla_wuk[l], (1, 2, 0)).astype(BF16),
        wuv_t=jnp.transpose(mla_wuv[l], (1, 0, 2)).astype(BF16))


PROMPT_TILES = dict(in_proj_tm=1024, conv_ts=256, conv_tc=512, gla_ts=256, prep_tm=256,
                    attn_tq=128, attn_tk=512, attn_heads_per_block=2, mix_tm=1024, ffn_tm=512)
SAMPLE_TILES = dict(conv_tc=256, gla_tb=8, prep_tm=256, attn_pages_per_chunk=32)
COL_TILE = 512
OUT_PROJ_TM = 256


def _finish_layer(x2, z, ys, p, *, mix_tm, ffn_tm):
    l = p["layer"]
    mix = branch_mix(ys, z, p["w_branch"], l, tm=mix_tm, tn=COL_TILE)
    x2 = out_proj_residual(mix, p["w_out"], l, x2, p["norms"][1], tm=OUT_PROJ_TM)
    return ffn(x2, p["norms"][2], p["w_gu"], p["w_down"], l, p["norms"][3], tm=ffn_tm, tf=COL_TILE)


def _prompt_layer(x2, bsz, seq, rope_tab, p):
    t = PROMPT_TILES
    z = norm_matmul(x2, p["norms"][0], p["w_in_t"], p["layer"], tm=t["in_proj_tm"], tn=NZ_TILE)
    z3 = z.reshape(bsz, seq, NZ)
    y_conv, conv_new = conv_prompt(z3, p["conv_w"], ts=t["conv_ts"], tc=t["conv_tc"])
    y_gla, gla_new = gla_prompt(z3, p["wa2"], p["ba"], p["onorm"], ts=t["gla_ts"])
    q, kv32, kvb = mla_prep(z3, rope_tab[0], rope_tab[1], p["q_norm"], p["kv_norm"], p["wuq"], p["wuk_t"],
                            tm=t["prep_tm"])
    y_mla = mla_prompt_attend(q, kvb, p["wuv_t"], tq=t["attn_tq"], tk=t["attn_tk"], hb=t["attn_heads_per_block"])
    ys = [y.reshape(bsz * seq, BRANCH_DIM) for y in (y_conv, y_gla, y_mla)]
    x2 = _finish_layer(x2, z, ys, p, mix_tm=t["mix_tm"], ffn_tm=t["ffn_tm"])
    return x2, kv32, gla_new, conv_new


def _sample_layer(x2, bsz, n_tok, rope_tab, conv_buf, gla_s0, cache_t, page_table, p):
    t = SAMPLE_TILES
    m = bsz * n_tok
    z = norm_matmul(x2, p["norms"][0], p["w_in_t"], p["layer"], tm=m, tn=NZ_TILE)
    buf_rows = jnp.repeat(conv_buf.reshape(bsz, (CONV_WIDTH - 1) * CONV_DIM), n_tok, axis=0)
    y_conv, u = conv_sample(z, buf_rows, p["conv_w"], n_tok=n_tok, tc=t["conv_tc"])
    conv_new = u.reshape(bsz, n_tok, CONV_DIM)[:, n_tok - (CONV_WIDTH - 1):, :]
    y_gla, gla_new = gla_sample(z, gla_s0, p["layer"], p["wa2"], p["ba"], p["onorm"], n_tok=n_tok, tb=t["gla_tb"])
    q, kv32, kvb = mla_prep(z.reshape(1, m, NZ), rope_tab[0], rope_tab[1], p["q_norm"], p["kv_norm"],
                            p["wuq"], p["wuk_t"], tm=t["prep_tm"])
    qs = q.reshape(MLA_HEADS, bsz, n_tok, MLA_QK).transpose(1, 0, 2, 3).reshape(bsz, MLA_HEADS * n_tok, MLA_QK)
    kt_new = jnp.pad(jnp.swapaxes(kvb.reshape(bsz, n_tok, MLA_QK), 1, 2), ((0, 0), (0, 0), (0, PAGE_SIZE - n_tok)))
    o_lat = mla_sample_attend(qs, kt_new, cache_t, page_table, layer=p["layer"], n_tok=n_tok,
                              cp=t["attn_pages_per_chunk"])
    o_h = o_lat.reshape(bsz, MLA_HEADS, n_tok, MLA_KV_RANK).transpose(1, 0, 2, 3).reshape(MLA_HEADS, m, MLA_KV_RANK)
    y_mla = head_matmul(o_h, p["wuv_t"])
    x2 = _finish_layer(x2, z, [y_conv, y_gla, y_mla], p, mix_tm=m, ffn_tm=m)
    return x2, kv32.reshape(bsz, n_tok, MLA_QK), gla_new, conv_new


def kernel(x_prompt, x_sample, cache_mla, page_table, state_gla, state_conv, norms, w_in, conv_w, gla_wa2, gla_ba, gla_onorm, mla_q_norm, mla_kv_norm, mla_wuq, mla_wuk, mla_wuv, w_branch, w_out, ffn_w_gu, ffn_w_down):
    bp, sp, d = x_prompt.shape
    bs, ss, _ = x_sample.shape
    depth = norms.shape[0]
    past_len = page_table.shape[1] * PAGE_SIZE
    tab_p = _rope_tables(np.arange(sp), MLA_HEADS)
    tab_s = _rope_tables(past_len + np.arange(bs * ss) % ss, MLA_HEADS)
    xp = x_prompt.reshape(bp * sp, d)
    xs = x_sample.reshape(bs * ss, d)
    cache_t = jnp.swapaxes(cache_mla, 2, 3)
    big = _stacked_weights(w_in, w_branch, w_out, ffn_w_gu, ffn_w_down)
    outs = [[] for _ in range(6)]
    for l in range(depth):
        p = _layer_weights(l, big, norms, conv_w, gla_wa2, gla_ba, gla_onorm, mla_q_norm, mla_kv_norm, mla_wuq,
                           mla_wuk, mla_wuv)
        xp, kv_p, gla_p, conv_p = _prompt_layer(xp, bp, sp, tab_p, p)
        xs, kv_s, gla_s, conv_s = _sample_layer(xs, bs, ss, tab_s, state_conv[l], state_gla, cache_t, page_table, p)
        for acc, val in zip(outs, (kv_p, kv_s, gla_p, gla_s, conv_p, conv_s)):
            acc.append(val)
    return (xp.reshape(bp, sp, d), xs.reshape(bs, ss, d)) + tuple(jnp.stack(o) for o in outs)
```

```python
import functools
import math

import numpy as np
import jax
import jax.numpy as jnp
from jax import lax
from jax.experimental import pallas as pl
from jax.experimental.pallas import tpu as pltpu

F32 = jnp.float32
BF16 = jnp.bfloat16

D_MODEL = 2048
BRANCH_DIM = 1024
N_BRANCH = 3
CONV_DIM = BRANCH_DIM
CONV_WIDTH = 3
GLA_HEADS = 4
GLA_DK = 128
GLA_DV = BRANCH_DIM // GLA_HEADS
GLA_LOWRANK = 16
GLA_TAU = 16.0
GLA_CHUNK = 64
MLA_HEADS = 8
MLA_Q_RANK = 512
MLA_KV_RANK = 256
MLA_NOPE = 128
MLA_ROPE = 64
MLA_V = BRANCH_DIM // MLA_HEADS
MLA_SCALE = (MLA_NOPE + MLA_ROPE) ** -0.5
MLA_QK = MLA_KV_RANK + MLA_ROPE
ROPE_THETA = 10000.0
PAGE_SIZE = 128
D_FF = -(-8 * D_MODEL // (3 * 256)) * 256
RMS_EPS = 1e-6
NEG_INF = -1e30

LANES = 128
SUBLANES = 8
VMEM_LIMIT_BYTES = 56 * 1024 * 1024

Z_GATES = 0
Z_CONV = Z_GATES + N_BRANCH * D_MODEL
Z_GQ = Z_CONV + 3 * CONV_DIM
Z_GK = Z_GQ + GLA_HEADS * GLA_DK
Z_GV = Z_GK + GLA_HEADS * GLA_DK
Z_GG = Z_GV + GLA_HEADS * GLA_DV
Z_MQ = Z_GG + GLA_HEADS * GLA_DV
Z_MKV = Z_MQ + MLA_Q_RANK
Z_MISC = Z_MKV + MLA_KV_RANK
Z_END = Z_MISC + LANES
NZ_TILE = 512
NZ = -(-Z_END // NZ_TILE) * NZ_TILE
MISC_KR = 0
MISC_GA = MLA_ROPE


def _cparams(*sem):
    return pltpu.CompilerParams(dimension_semantics=sem, vmem_limit_bytes=VMEM_LIMIT_BYTES)


def _rms_rows(x, g):
    ms = jnp.mean(x * x, axis=-1, keepdims=True)
    return x * lax.rsqrt(ms + RMS_EPS) * g


def _norm_rows_to(dst_ref, src_ref, g_ref, rows, chunk=32):
    def body(c, carry):
        r0 = pl.multiple_of(c * chunk, chunk)
        dst_ref[pl.ds(r0, chunk), :] = _rms_rows(src_ref[pl.ds(r0, chunk), :], g_ref[...]).astype(dst_ref.dtype)
        return carry
    lax.fori_loop(0, rows // chunk, body, 0)


def _norm_mm_kernel(x_ref, g_ref, w_ref, o_ref, xn_ref):
    @pl.when(pl.program_id(1) == 0)
    def _():
        _norm_rows_to(xn_ref, x_ref, g_ref, x_ref.shape[0])
    o_ref[...] = _dot_nt(xn_ref[...], w_ref[...])


def norm_matmul(x, g, w_t, layer, *, tm, tn):
    m, k = x.shape
    n = w_t.shape[1]
    return pl.pallas_call(
        _norm_mm_kernel,
        grid=(m // tm, n // tn),
        in_specs=[pl.BlockSpec((tm, k), lambda i, j: (i, 0)),
                  pl.BlockSpec((1, k), lambda i, j: (0, 0)),
                  pl.BlockSpec((None, tn, k), lambda i, j: (layer, j, 0))],
        out_specs=pl.BlockSpec((tm, tn), lambda i, j: (i, j)),
        out_shape=jax.ShapeDtypeStruct((m, n), F32),
        scratch_shapes=[pltpu.VMEM((tm, k), BF16)],
        compiler_params=_cparams("parallel", "arbitrary"),
        name="in_proj",
    )(x, g.reshape(1, k), w_t)


def _conv_prompt_kernel(cb_ref, cc_ref, ch_ref, w_ref, y_ref, st_ref, carry_ref):
    i = pl.program_id(2)

    @pl.when(i == 0)
    def _():
        carry_ref[...] = jnp.zeros_like(carry_ref)

    u = cc_ref[...] * ch_ref[...]
    ts = u.shape[0]
    row = lax.broadcasted_iota(jnp.int32, u.shape, 0)
    prev1 = carry_ref[1:2, :]
    prev2 = carry_ref[0:1, :]
    um1 = jnp.where(row == 0, prev1, pltpu.roll(u, 1, axis=0))
    um2 = jnp.where(row == 0, prev2, jnp.where(row == 1, prev1, pltpu.roll(u, 2, axis=0)))
    w = w_ref[...]
    y = um2 * w[0:1, :] + um1 * w[1:2, :] + u * w[2:3, :]
    y_ref[...] = (cb_ref[...] * y).astype(y_ref.dtype)
    last = u[ts - 2:ts, :]
    carry_ref[...] = last

    @pl.when(i == pl.num_programs(2) - 1)
    def _():
        st_ref[...] = last


def conv_prompt(z3, conv_w, *, ts, tc):
    b, s, _ = z3.shape
    c = CONV_DIM
    base = Z_CONV // tc
    nc = c // tc

    def zspec(part):
        return pl.BlockSpec((None, ts, tc), lambda bi, ci, i, part=part: (bi, i, base + part * nc + ci))

    return pl.pallas_call(
        _conv_prompt_kernel,
        grid=(b, nc, s // ts),
        in_specs=[zspec(0), zspec(1), zspec(2),
                  pl.BlockSpec((CONV_WIDTH, tc), lambda bi, ci, i: (0, ci))],
        out_specs=[pl.BlockSpec((None, ts, tc), lambda bi, ci, i: (bi, i, ci)),
                   pl.BlockSpec((None, CONV_WIDTH - 1, tc), lambda bi, ci, i: (bi, 0, ci))],
        out_shape=[jax.ShapeDtypeStruct((b, s, c), BF16),
                   jax.ShapeDtypeStruct((b, CONV_WIDTH - 1, c), F32)],
        scratch_shapes=[pltpu.VMEM((CONV_WIDTH - 1, tc), F32)],
        compiler_params=_cparams("parallel", "parallel", "arbitrary"),
        name="conv_prompt",
    )(z3, z3, z3, conv_w)


def _conv_sample_kernel(cb_ref, cc_ref, ch_ref, b0_ref, b1_ref, w_ref, y_ref, u_ref, *, n_tok):
    u = cc_ref[...] * ch_ref[...]
    tpos = lax.broadcasted_iota(jnp.int32, u.shape, 0) % n_tok
    prev1 = b1_ref[...]
    um1 = jnp.where(tpos == 0, prev1, pltpu.roll(u, 1, axis=0))
    um2 = jnp.where(tpos == 0, b0_ref[...], jnp.where(tpos == 1, prev1, pltpu.roll(u, 2, axis=0)))
    w = w_ref[...]
    y = um2 * w[0:1, :] + um1 * w[1:2, :] + u * w[2:3, :]
    y_ref[...] = cb_ref[...] * y
    u_ref[...] = u


def conv_sample(z2, buf_rows, conv_w, *, n_tok, tc):
    m = z2.shape[0]
    c = CONV_DIM
    nc = c // tc
    base = Z_CONV // tc

    def zspec(part):
        return pl.BlockSpec((m, tc), lambda ci, part=part: (0, base + part * nc + ci))

    return pl.pallas_call(
        functools.partial(_conv_sample_kernel, n_tok=n_tok),
        grid=(nc,),
        in_specs=[zspec(0), zspec(1), zspec(2),
                  pl.BlockSpec((m, tc), lambda ci: (0, ci)),
                  pl.BlockSpec((m, tc), lambda ci: (0, nc + ci)),
                  pl.BlockSpec((CONV_WIDTH, tc), lambda ci: (0, ci))],
        out_specs=[pl.BlockSpec((m, tc), lambda ci: (0, ci)), pl.BlockSpec((m, tc), lambda ci: (0, ci))],
        out_shape=[jax.ShapeDtypeStruct((m, c), F32), jax.ShapeDtypeStruct((m, c), F32)],
        compiler_params=_cparams("parallel"),
        name="conv_sample",
    )(z2, z2, z2, buf_rows, buf_rows, conv_w)


def _log_sigmoid(x):
    return jnp.minimum(x, 0.0) - jnp.log1p(jnp.exp(-jnp.abs(x)))


def _split3(x):
    hi = x.astype(BF16)
    r1 = x - hi.astype(F32)
    mid = r1.astype(BF16)
    lo = (r1 - mid.astype(F32)).astype(BF16)
    return hi, mid, lo


def _col_bcast(row, width):
    n = row.shape[1]
    sq = jnp.transpose(jnp.broadcast_to(row, (n, n)))
    return jnp.concatenate([sq] * (width // n), axis=1)


def _dot_nt(a, b):
    return lax.dot_general(a, b, (((1,), (1,)), ((), ())), preferred_element_type=F32)


def _dot_tn(a, b):
    return lax.dot_general(a, b, (((0,), (0,)), ((), ())), preferred_element_type=F32)


def _gla_gate_norm(o, g, onorm):
    on = o * lax.rsqrt(jnp.mean(o * o, axis=-1, keepdims=True) + RMS_EPS) * onorm
    return on * (g * jax.nn.sigmoid(g))


def _pad_rows(x, rows):
    if x.shape[0] == rows:
        return x
    return jnp.concatenate([x, jnp.zeros((rows - x.shape[0], x.shape[1]), x.dtype)], axis=0)


def _gla_log_decay(misc, wa2_ref, ba_ref):
    lane = lax.broadcasted_iota(jnp.int32, misc.shape, 1)
    a_low = jnp.where((lane >= MISC_GA) & (lane < MISC_GA + GLA_LOWRANK), misc, 0.0).astype(BF16)
    a = jnp.dot(a_low, wa2_ref[...], preferred_element_type=F32) + ba_ref[...]
    return _log_sigmoid(a) * (1.0 / GLA_TAU)


def _gla_prompt_kernel(q_ref, k_ref, v_ref, g_ref, a_ref, wa2_ref, ba_ref, on_ref, y_ref, st_ref, s_ref, *, ts):
    i = pl.program_id(1)

    @pl.when(i == 0)
    def _():
        s_ref[...] = jnp.zeros_like(s_ref)

    c = GLA_CHUNK
    pc = LANES
    rr = lax.broadcasted_iota(jnp.int32, (c, pc), 0)
    cc = lax.broadcasted_iota(jnp.int32, (c, pc), 1)
    tril = rr >= cc
    tril_b = tril.astype(BF16)
    scale = GLA_DK ** -0.5
    for ci in range(ts // c):
        r = slice(ci * c, (ci + 1) * c)
        la = _gla_log_decay(a_ref[r, :], wa2_ref, ba_ref)
        big_l = None
        for term in _split3(_pad_rows(la, pc)):
            part = jnp.dot(tril_b, term, preferred_element_type=F32)
            big_l = part if big_l is None else big_l + part
        l_last = big_l[c - 1:c, :]
        e_pos = jnp.exp(big_l)
        e_neg = jnp.exp(-big_l)
        e_dec = jnp.exp(l_last - big_l)
        e_last = jnp.exp(l_last)
        heads = range(GLA_HEADS)
        hs = [slice(h * GLA_DK, (h + 1) * GLA_DK) for h in heads]
        vs = [slice(h * GLA_DV, (h + 1) * GLA_DV) for h in heads]
        ks = [k_ref[r, hs[h]] for h in heads]
        q_t = [(q_ref[r, hs[h]] * scale * e_pos[:, hs[h]]).astype(BF16) for h in heads]
        attn_raw = [_dot_nt(q_t[h], _pad_rows(ks[h] * e_neg[:, hs[h]], pc).astype(BF16)) for h in heads]
        v = [_pad_rows(v_ref[r, vs[h]], pc).astype(BF16) for h in heads]
        s_old = [s_ref[h] for h in heads]
        o_inter = [jnp.dot(q_t[h], s_old[h].astype(BF16), preferred_element_type=F32) for h in heads]
        s_add = [_dot_tn(_pad_rows(ks[h] * e_dec[:, hs[h]], pc).astype(BF16), v[h]) for h in heads]
        for h in heads:
            attn = jnp.where(tril, attn_raw[h], 0.0).astype(BF16)
            o = o_inter[h] + jnp.dot(attn, v[h], preferred_element_type=F32)
            s_ref[h] = s_old[h] * _col_bcast(e_last[:, hs[h]], GLA_DV) + s_add[h]
            y_ref[r, vs[h]] = _gla_gate_norm(o, g_ref[r, vs[h]], on_ref[...]).astype(y_ref.dtype)

    @pl.when(i == pl.num_programs(1) - 1)
    def _():
        st_ref[...] = s_ref[...]


def gla_prompt(z3, wa2, ba, onorm, *, ts):
    b, s, _ = z3.shape
    hk = GLA_HEADS * GLA_DK
    hv = GLA_HEADS * GLA_DV
    return pl.pallas_call(
        functools.partial(_gla_prompt_kernel, ts=ts),
        grid=(b, s // ts),
        in_specs=[pl.BlockSpec((None, ts, hk), lambda bi, i: (bi, i, Z_GQ // hk)),
                  pl.BlockSpec((None, ts, hk), lambda bi, i: (bi, i, Z_GK // hk)),
                  pl.BlockSpec((None, ts, hv), lambda bi, i: (bi, i, Z_GV // hv)),
                  pl.BlockSpec((None, ts, hv), lambda bi, i: (bi, i, Z_GG // hv)),
                  pl.BlockSpec((None, ts, LANES), lambda bi, i: (bi, i, Z_MISC // LANES)),
                  pl.BlockSpec((LANES, hk), lambda bi, i: (0, 0)),
                  pl.BlockSpec((1, hk), lambda bi, i: (0, 0)),
                  pl.BlockSpec((1, GLA_DV), lambda bi, i: (0, 0))],
        out_specs=[pl.BlockSpec((None, ts, hv), lambda bi, i: (bi, i, 0)),
                   pl.BlockSpec((None, GLA_HEADS, GLA_DK, GLA_DV), lambda bi, i: (bi, 0, 0, 0))],
        out_shape=[jax.ShapeDtypeStruct((b, s, hv), BF16),
                   jax.ShapeDtypeStruct((b, GLA_HEADS, GLA_DK, GLA_DV), F32)],
        scratch_shapes=[pltpu.VMEM((GLA_HEADS, GLA_DK, GLA_DV), F32)],
        compiler_params=_cparams("parallel", "arbitrary"),
        name="gla_prompt",
    )(z3, z3, z3, z3, z3, wa2, ba.reshape(1, hk), onorm.reshape(1, GLA_DV))


def _gla_sample_kernel(q_ref, k_ref, v_ref, g_ref, a_ref, wa2_ref, ba_ref, on_ref, s0_ref, y_ref, st_ref,
                       *, n_tok, n_groups):
    rows = SUBLANES
    per = rows // n_tok
    pr = 2 * SUBLANES
    pc = LANES
    rr = lax.broadcasted_iota(jnp.int32, (pr, pc), 0)
    cc = lax.broadcasted_iota(jnp.int32, (pr, pc), 1)
    causal = (rr // n_tok == cc // n_tok) & (rr >= cc)
    trow = lax.broadcasted_iota(jnp.int32, (rows, 1), 0)
    tpos = trow % n_tok
    scale = GLA_DK ** -0.5

    def body(p, carry):
        r0 = pl.multiple_of(p * rows, rows)
        r = pl.ds(r0, rows)
        la = _gla_log_decay(a_ref[r, :], wa2_ref, ba_ref)
        big_l = la
        for d in range(1, n_tok):
            big_l = big_l + jnp.where(tpos >= d, pltpu.roll(la, d, axis=0), 0.0)
        l_last = jnp.zeros_like(big_l)
        for j in range(per):
            last_j = big_l[(j + 1) * n_tok - 1:(j + 1) * n_tok, :]
            l_last = jnp.where(trow // n_tok == j, last_j, l_last)
        e_pos = jnp.exp(big_l)
        e_neg = jnp.exp(-big_l)
        e_dec = jnp.exp(l_last - big_l)
        for h in range(GLA_HEADS):
            hs = slice(h * GLA_DK, (h + 1) * GLA_DK)
            vs = slice(h * GLA_DV, (h + 1) * GLA_DV)
            k = k_ref[r, hs]
            q_t = _pad_rows(q_ref[r, hs] * scale * e_pos[:, hs], pr).astype(BF16)
            k_t = _pad_rows(k * e_neg[:, hs], pc).astype(BF16)
            k_dec = k * e_dec[:, hs]
            v = _pad_rows(v_ref[r, vs], pc).astype(BF16)
            attn = jnp.where(causal, _dot_nt(q_t, k_t), 0.0).astype(BF16)
            o = jnp.dot(attn, v, preferred_element_type=F32)[0:rows]
            for j in range(per):
                seq = p * per + j
                in_seq = trow // n_tok == j
                s_old = s0_ref[seq, h]
                o_inter = jnp.dot(q_t, s_old.astype(BF16), preferred_element_type=F32)[0:rows]
                o = o + jnp.where(in_seq, o_inter, 0.0)
                e_last = jnp.exp(big_l[(j + 1) * n_tok - 1:(j + 1) * n_tok, hs])
                k_dec_j = _pad_rows(jnp.where(in_seq, k_dec, 0.0), pc).astype(BF16)
                st_ref[seq, h] = s_old * _col_bcast(e_last, GLA_DV) + _dot_tn(k_dec_j, v)
            y_ref[r, vs] = _gla_gate_norm(o, g_ref[r, vs], on_ref[...])
        return carry

    lax.fori_loop(0, n_groups, body, 0)


def gla_sample(z2, s0_all, layer, wa2, ba, onorm, *, n_tok, tb):
    m = z2.shape[0]
    b = m // n_tok
    hk = GLA_HEADS * GLA_DK
    hv = GLA_HEADS * GLA_DV
    tr = tb * n_tok
    st_spec = pl.BlockSpec((tb, GLA_HEADS, GLA_DK, GLA_DV), lambda i: (i, 0, 0, 0))
    s0_spec = pl.BlockSpec((None, tb, GLA_HEADS, GLA_DK, GLA_DV), lambda i: (layer, i, 0, 0, 0))
    return pl.pallas_call(
        functools.partial(_gla_sample_kernel, n_tok=n_tok, n_groups=tr // SUBLANES),
        grid=(b // tb,),
        in_specs=[pl.BlockSpec((tr, hk), lambda i: (i, Z_GQ // hk)),
                  pl.BlockSpec((tr, hk), lambda i: (i, Z_GK // hk)),
                  pl.BlockSpec((tr, hv), lambda i: (i, Z_GV // hv)),
                  pl.BlockSpec((tr, hv), lambda i: (i, Z_GG // hv)),
                  pl.BlockSpec((tr, LANES), lambda i: (i, Z_MISC // LANES)),
                  pl.BlockSpec((LANES, hk), lambda i: (0, 0)),
                  pl.BlockSpec((1, hk), lambda i: (0, 0)),
                  pl.BlockSpec((1, GLA_DV), lambda i: (0, 0)),
                  s0_spec],
        out_specs=[pl.BlockSpec((tr, hv), lambda i: (i, 0)), st_spec],
        out_shape=[jax.ShapeDtypeStruct((m, hv), F32),
                   jax.ShapeDtypeStruct(s0_all.shape[1:], F32)],
        compiler_params=_cparams("parallel"),
        name="gla_sample",
    )(z2, z2, z2, z2, z2, wa2, ba.reshape(1, hk), onorm.reshape(1, GLA_DV), s0_all)


def _rope_tables(pos, reps):
    half = MLA_ROPE // 2
    inv_freq = ROPE_THETA ** (-np.arange(half, dtype=np.float64) / half)
    ang = np.asarray(pos, np.float64)[:, None] * inv_freq[None, :]
    cos = np.concatenate([np.cos(ang), np.cos(ang)], axis=1)
    sin = np.concatenate([-np.sin(ang), np.sin(ang)], axis=1)
    return (jnp.asarray(np.tile(cos, (1, reps)), F32), jnp.asarray(np.tile(sin, (1, reps)), F32))


def _rope(x, cos, sin):
    n = x.shape[1]
    half = MLA_ROPE // 2
    lane = lax.broadcasted_iota(jnp.int32, x.shape, 1)
    partner = jnp.where(lane % MLA_ROPE < half, pltpu.roll(x, n - half, axis=1), pltpu.roll(x, half, axis=1))
    return x * cos + partner * sin


def _mla_prep_kernel(mq_ref, mkv_ref, misc_ref, cos_ref, sin_ref, qn_ref, kvn_ref, wuq_ref, wuk_ref,
                     q_out, kv32_out, kvb_out, cq_ref, q_ref):
    tm = mq_ref.shape[0]
    _norm_rows_to(cq_ref, mq_ref, qn_ref, tm)
    q_ref[...] = jnp.dot(cq_ref[...], wuq_ref[...], preferred_element_type=F32)
    n_nope = MLA_HEADS * MLA_NOPE
    q_rope = _rope(q_ref[:, n_nope:], cos_ref[...], sin_ref[...])
    for h in range(MLA_HEADS):
        q_nope = q_ref[:, h * MLA_NOPE:(h + 1) * MLA_NOPE].astype(BF16)
        q_out[h, :, 0:MLA_KV_RANK] = jnp.dot(q_nope, wuk_ref[h], preferred_element_type=F32).astype(BF16)
        q_out[h, :, MLA_KV_RANK:MLA_QK] = q_rope[:, h * MLA_ROPE:(h + 1) * MLA_ROPE].astype(BF16)
    c = _rms_rows(mkv_ref[...], kvn_ref[...])
    kr = _rope(misc_ref[...], cos_ref[:, 0:LANES], sin_ref[:, 0:LANES])[:, MISC_KR:MISC_KR + MLA_ROPE]
    kv32_out[:, 0:MLA_KV_RANK] = c
    kv32_out[:, MLA_KV_RANK:MLA_QK] = kr
    kvb_out[:, 0:MLA_KV_RANK] = c.astype(BF16)
    kvb_out[:, MLA_KV_RANK:MLA_QK] = kr.astype(BF16)


def mla_prep(z3, cos, sin, q_norm, kv_norm, wuq, wuk_t, *, tm):
    b, s, _ = z3.shape
    nq = MLA_HEADS * (MLA_NOPE + MLA_ROPE)
    nr = MLA_HEADS * MLA_ROPE
    const2 = lambda bi, i: (0, 0)
    return pl.pallas_call(
        _mla_prep_kernel,
        grid=(b, s // tm),
        in_specs=[pl.BlockSpec((None, tm, MLA_Q_RANK), lambda bi, i: (bi, i, Z_MQ // MLA_Q_RANK)),
                  pl.BlockSpec((None, tm, MLA_KV_RANK), lambda bi, i: (bi, i, Z_MKV // MLA_KV_RANK)),
                  pl.BlockSpec((None, tm, LANES), lambda bi, i: (bi, i, Z_MISC // LANES)),
                  pl.BlockSpec((tm, nr), lambda bi, i: (i, 0)),
                  pl.BlockSpec((tm, nr), lambda bi, i: (i, 0)),
                  pl.BlockSpec((1, MLA_Q_RANK), const2),
                  pl.BlockSpec((1, MLA_KV_RANK), const2),
                  pl.BlockSpec((MLA_Q_RANK, nq), const2),
                  pl.BlockSpec((MLA_HEADS, MLA_NOPE, MLA_KV_RANK), lambda bi, i: (0, 0, 0))],
        out_specs=[pl.BlockSpec((None, MLA_HEADS, tm, MLA_QK), lambda bi, i: (bi, 0, i, 0)),
                   pl.BlockSpec((None, tm, MLA_QK), lambda bi, i: (bi, i, 0)),
                   pl.BlockSpec((None, tm, MLA_QK), lambda bi, i: (bi, i, 0))],
        out_shape=[jax.ShapeDtypeStruct((b, MLA_HEADS, s, MLA_QK), BF16),
                   jax.ShapeDtypeStruct((b, s, MLA_QK), F32),
                   jax.ShapeDtypeStruct((b, s, MLA_QK), BF16)],
        scratch_shapes=[pltpu.VMEM((tm, MLA_Q_RANK), BF16), pltpu.VMEM((tm, nq), F32)],
        compiler_params=_cparams("parallel", "parallel"),
        name="mla_prep",
    )(z3, z3, z3, cos, sin, q_norm.reshape(1, -1), kv_norm.reshape(1, -1), wuq, wuk_t)


SOFTMAX_EXP2_SCALE = MLA_SCALE * math.log2(math.e)


def _online_softmax(s, kv_c_t, m_prev, l_prev, acc_prev, contract_kv_lanes):
    m_new = jnp.maximum(m_prev, jnp.max(s, axis=-1, keepdims=True))
    corr = jnp.exp2((m_prev - m_new) * SOFTMAX_EXP2_SCALE)
    p = jnp.exp2((s - m_new) * SOFTMAX_EXP2_SCALE)
    pv = _dot_nt(p.astype(BF16), kv_c_t) if contract_kv_lanes else jnp.dot(
        p.astype(BF16), kv_c_t, preferred_element_type=F32)
    return m_new, l_prev * corr + jnp.sum(p, axis=-1, keepdims=True), acc_prev * corr + pv


def _mla_prompt_kernel(q_ref, kv_ref, wuv_ref, y_ref, m_ref, l_ref, acc_ref,
                       s0_ref, s1_ref, p0_ref, p1_ref, c0_ref, c1_ref, *, tq, tk, hb):
    i = pl.program_id(1)
    rb = hb * tq
    n_blocks = MLA_HEADS // hb
    shift = n_blocks.bit_length() - 1
    assert n_blocks == 1 << shift
    m_ref[...] = jnp.full_like(m_ref, NEG_INF)
    l_ref[...] = jnp.zeros_like(l_ref)
    acc_ref[...] = jnp.zeros_like(acc_ref)
    n_items = ((i * tq + tq - 1) // tk + 1) * n_blocks

    def key_rows(t):
        return pl.ds(pl.multiple_of(lax.shift_right_logical(t, shift) * tk, tk), tk)

    def scores(t, s_ref):
        r = t & (n_blocks - 1)
        q = q_ref[pl.ds(r * hb, hb)].reshape(rb, MLA_QK)
        s = _dot_nt(q, kv_ref[key_rows(t), :])
        qpos = i * tq + lax.broadcasted_iota(jnp.int32, s.shape, 0) % tq
        kpos = lax.shift_right_logical(t, shift) * tk + lax.broadcasted_iota(jnp.int32, s.shape, 1)
        s_ref[...] = jnp.where(kpos <= qpos, s, NEG_INF)

    def softmax_stats(t, s_ref, p_ref, c_ref):
        r = t & (n_blocks - 1)
        s = s_ref[...]
        m_prev = m_ref[r]
        m_new = jnp.maximum(m_prev, jnp.max(s, axis=-1, keepdims=True))
        corr = jnp.exp2((m_prev - m_new) * SOFTMAX_EXP2_SCALE)
        p = jnp.exp2((s - m_new) * SOFTMAX_EXP2_SCALE)
        l_ref[r] = l_ref[r] * corr + jnp.sum(p, axis=-1, keepdims=True)
        m_ref[r] = m_new
        p_ref[...] = p.astype(BF16)
        c_ref[...] = corr

    def accumulate(t, p_ref, c_ref):
        r = t & (n_blocks - 1)
        kv_c = kv_ref[key_rows(t), 0:MLA_KV_RANK]
        acc_ref[r] = acc_ref[r] * c_ref[...] + jnp.dot(p_ref[...], kv_c, preferred_element_type=F32)

    assert n_blocks % 2 == 0 and n_blocks >= 4
    buf0, buf1 = (s0_ref, p0_ref, c0_ref), (s1_ref, p1_ref, c1_ref)

    def step(t, cur, nxt, with_scores=True):
        if with_scores:
            scores(t + 1, nxt[0])
        softmax_stats(t, *cur)
        accumulate(t - 1, nxt[1], nxt[2])

    def step_pair(u, carry):
        t = 2 * u + 1
        step(t, buf1, buf0)
        step(t + 1, buf0, buf1)
        return carry

    scores(jnp.int32(0), s0_ref)
    scores(jnp.int32(1), s1_ref)
    softmax_stats(jnp.int32(0), *buf0)
    lax.fori_loop(0, lax.shift_right_logical(n_items, 1) - 1, step_pair, 0)
    step(n_items - 1, buf1, buf0, with_scores=False)
    accumulate(n_items - 1, p1_ref, c1_ref)

    for h in range(MLA_HEADS):
        r, hs = h // hb, slice((h % hb) * tq, (h % hb + 1) * tq)
        o = (acc_ref[r, hs, :] / l_ref[r, hs, :]).astype(BF16)
        y_ref[:, h * MLA_V:(h + 1) * MLA_V] = jnp.dot(o, wuv_ref[h], preferred_element_type=F32).astype(y_ref.dtype)


def mla_prompt_attend(q, kvb, wuv_t, *, tq, tk, hb):
    b, _, s, _ = q.shape
    rb = hb * tq
    n_blocks = MLA_HEADS // hb
    return pl.pallas_call(
        functools.partial(_mla_prompt_kernel, tq=tq, tk=tk, hb=hb),
        grid=(b, s // tq),
        in_specs=[pl.BlockSpec((None, MLA_HEADS, tq, MLA_QK), lambda bi, i: (bi, 0, i, 0)),
                  pl.BlockSpec((None, s, MLA_QK), lambda bi, i: (bi, 0, 0)),
                  pl.BlockSpec((MLA_HEADS, MLA_KV_RANK, MLA_V), lambda bi, i: (0, 0, 0))],
        out_specs=pl.BlockSpec((None, tq, BRANCH_DIM), lambda bi, i: (bi, i, 0)),
        out_shape=jax.ShapeDtypeStruct((b, s, BRANCH_DIM), BF16),
        scratch_shapes=[pltpu.VMEM((n_blocks, rb, 1), F32), pltpu.VMEM((n_blocks, rb, 1), F32),
                        pltpu.VMEM((n_blocks, rb, MLA_KV_RANK), F32),
                        pltpu.VMEM((rb, tk), F32), pltpu.VMEM((rb, tk), F32),
                        pltpu.VMEM((rb, tk), BF16), pltpu.VMEM((rb, tk), BF16),
                        pltpu.VMEM((rb, 1), F32), pltpu.VMEM((rb, 1), F32)],
        compiler_params=_cparams("parallel", "arbitrary"),
        name="mla_prompt_attn",
    )(q, kvb, wuv_t)


def _lane_row(col):
    rows = col.shape[0]
    sq = jnp.transpose(jnp.broadcast_to(_pad_rows(col, LANES), (LANES, LANES)))
    return sq[0:1, 0:rows]


def _absorb_keys(s, kt_c, carry):
    m_prev, l_prev, acc_t = carry
    m_new = jnp.maximum(m_prev, jnp.max(s, axis=-1, keepdims=True))
    corr = jnp.exp2((m_prev - m_new) * SOFTMAX_EXP2_SCALE)
    p = jnp.exp2((s - m_new) * SOFTMAX_EXP2_SCALE)
    l_new = l_prev * corr + jnp.sum(p, axis=-1, keepdims=True)
    return m_new, l_new, acc_t * _lane_row(corr) + _dot_nt(kt_c, p.astype(BF16))


def _mla_sample_kernel(pt_ref, q_ref, new_ref, cache_ref, o_ref, f0, f1, h0, h1, s0, s1, sem,
                       *, layer, n_tok, n_chunks, cp):
    b = pl.program_id(0)
    nb = pl.num_programs(0)
    fbuf, hbuf, sbuf = (f0, f1), (h0, h1), (s0, s1)

    def page_copy(bi, c, p, slot):
        page = pt_ref[bi, c * cp + p]
        return pltpu.make_async_copy(cache_ref.at[layer, page], fbuf[slot].at[p], sem.at[slot])

    def start_chunk(bi, c, slot):
        for p in range(cp):
            page_copy(bi, c, p, slot).start()

    def start_ahead(c, slot):
        @pl.when(c < n_chunks)
        def _():
            start_chunk(b, c, slot)

        @pl.when((c >= n_chunks) & (b + 1 < nb))
        def _():
            start_chunk(b + 1, c - n_chunks, slot)

    def score_chunk(c, slot):
        for p in range(cp):
            page_copy(b, c, p, slot).wait()
        for p in range(cp):
            hbuf[slot][:, p * PAGE_SIZE:(p + 1) * PAGE_SIZE] = fbuf[slot][p].astype(BF16)
        sbuf[slot][...] = jnp.dot(q, hbuf[slot][...], preferred_element_type=F32)

    def absorb_chunk(slot, carry):
        return _absorb_keys(sbuf[slot][...], hbuf[slot][0:MLA_KV_RANK, :], carry)

    @pl.when(b == 0)
    def _():
        start_chunk(b, 0, 0)
        start_chunk(b, 1, 1)

    q = q_ref[...]
    rows = q.shape[0]
    score_chunk(0, 0)

    def chunk_pair(c2, carry):
        c = 2 * c2
        start_ahead(c + 2, 0)
        score_chunk(c + 1, 1)
        carry = absorb_chunk(0, carry)
        start_ahead(c + 3, 1)
        score_chunk(c + 2, 0)
        return absorb_chunk(1, carry)

    carry = (jnp.full((rows, 1), NEG_INF, F32), jnp.zeros((rows, 1), F32), jnp.zeros((MLA_KV_RANK, rows), F32))
    carry = lax.fori_loop(0, n_chunks // 2 - 1, chunk_pair, carry)
    start_ahead(n_chunks, 0)
    score_chunk(n_chunks - 1, 1)
    carry = absorb_chunk(0, carry)
    start_ahead(n_chunks + 1, 1)
    carry = absorb_chunk(1, carry)

    kt_new = new_ref[...]
    s = jnp.dot(q, kt_new, preferred_element_type=F32)
    tpos = lax.broadcasted_iota(jnp.int32, s.shape, 0) % n_tok
    kpos = lax.broadcasted_iota(jnp.int32, s.shape, 1)
    m, l, acc_t = _absorb_keys(jnp.where(kpos <= tpos, s, NEG_INF), kt_new[0:MLA_KV_RANK, :], carry)
    o_t = acc_t / _lane_row(l)
    o_sq = jnp.transpose(jnp.concatenate([o_t, jnp.zeros((MLA_KV_RANK, LANES - rows), F32)], axis=1))
    o_ref[...] = o_sq[0:rows, :].astype(o_ref.dtype)


def mla_sample_attend(q, kt_new_pad, cache_t, page_table, *, layer, n_tok, cp):
    b, rows, _ = q.shape
    n_pages = page_table.shape[1]
    n_chunks = n_pages // cp
    keys = cp * PAGE_SIZE
    assert n_chunks % 2 == 0 and n_chunks * cp == n_pages and rows <= LANES
    grid_spec = pltpu.PrefetchScalarGridSpec(
        num_scalar_prefetch=1,
        grid=(b,),
        in_specs=[pl.BlockSpec((None, rows, MLA_QK), lambda bi, pt: (bi, 0, 0)),
                  pl.BlockSpec((None, MLA_QK, PAGE_SIZE), lambda bi, pt: (bi, 0, 0)),
                  pl.BlockSpec(memory_space=pl.ANY)],
        out_specs=pl.BlockSpec((None, rows, MLA_KV_RANK), lambda bi, pt: (bi, 0, 0)),
        scratch_shapes=[pltpu.VMEM((cp, MLA_QK, PAGE_SIZE), F32), pltpu.VMEM((cp, MLA_QK, PAGE_SIZE), F32),
                        pltpu.VMEM((MLA_QK, keys), BF16), pltpu.VMEM((MLA_QK, keys), BF16),
                        pltpu.VMEM((rows, keys), F32), pltpu.VMEM((rows, keys), F32),
                        pltpu.SemaphoreType.DMA((2,))],
    )
    return pl.pallas_call(
        functools.partial(_mla_sample_kernel, layer=layer, n_tok=n_tok, n_chunks=n_chunks, cp=cp),
        grid_spec=grid_spec,
        out_shape=jax.ShapeDtypeStruct((b, rows, MLA_KV_RANK), BF16),
        compiler_params=_cparams("arbitrary"),
        name="mla_sample_attn",
    )(page_table, q, kt_new_pad, cache_t)


def _head_mm_kernel(o_ref, w_ref, y_ref):
    y_ref[...] = jnp.dot(o_ref[...], w_ref[...], preferred_element_type=F32).astype(y_ref.dtype)


def head_matmul(o, w):
    h, m, k = o.shape
    n = w.shape[2]
    return pl.pallas_call(
        _head_mm_kernel,
        grid=(h,),
        in_specs=[pl.BlockSpec((None, m, k), lambda i: (i, 0, 0)),
                  pl.BlockSpec((None, k, n), lambda i: (i, 0, 0))],
        out_specs=pl.BlockSpec((m, n), lambda i: (0, i)),
        out_shape=jax.ShapeDtypeStruct((m, h * n), BF16),
        compiler_params=_cparams("parallel"),
        name="mla_value_up",
    )(o, w)


def _branch_mix_kernel(y0_ref, y1_ref, y2_ref, g0_ref, g1_ref, g2_ref, w_ref, o_ref):
    mix = None
    for n, (y_ref, g_ref) in enumerate(((y0_ref, g0_ref), (y1_ref, g1_ref), (y2_ref, g2_ref))):
        proj = jnp.dot(y_ref[...].astype(BF16), w_ref[n], preferred_element_type=F32)
        term = jax.nn.sigmoid(g_ref[...]) * proj
        mix = term if mix is None else mix + term
    o_ref[...] = mix.astype(o_ref.dtype)


def branch_mix(ys, z, w_branch, layer, *, tm, tn):
    m = z.shape[0]
    per = D_MODEL // tn
    yspec = pl.BlockSpec((tm, BRANCH_DIM), lambda i, j: (i, 0))

    def gspec(n):
        return pl.BlockSpec((tm, tn), lambda i, j, n=n: (i, Z_GATES // tn + n * per + j))

    return pl.pallas_call(
        _branch_mix_kernel,
        grid=(m // tm, per),
        in_specs=[yspec, yspec, yspec, gspec(0), gspec(1), gspec(2),
                  pl.BlockSpec((None, N_BRANCH, BRANCH_DIM, tn), lambda i, j: (layer, 0, 0, j))],
        out_specs=pl.BlockSpec((tm, tn), lambda i, j: (i, j)),
        out_shape=jax.ShapeDtypeStruct((m, D_MODEL), BF16),
        compiler_params=_cparams("parallel", "arbitrary"),
        name="branch_mix",
    )(*ys, z, z, z, w_branch)


def _add_norm_rows(o_ref, x_ref, f_ref, g_ref, rows, chunk=32):
    def body(c, carry):
        r = pl.ds(pl.multiple_of(c * chunk, chunk), chunk)
        o_ref[r, :] = x_ref[r, :] + _rms_rows(f_ref[r, :], g_ref[...])
        return carry
    lax.fori_loop(0, rows // chunk, body, 0)


def _out_proj_kernel(mix_ref, w_ref, x_ref, g_ref, o_ref, f_ref):
    f_ref[...] = jnp.dot(mix_ref[...], w_ref[...], preferred_element_type=F32)
    _add_norm_rows(o_ref, x_ref, f_ref, g_ref, x_ref.shape[0])


def out_proj_residual(mix, w_out, layer, x, g, *, tm):
    m, d = x.shape
    return pl.pallas_call(
        _out_proj_kernel,
        grid=(m // tm,),
        in_specs=[pl.BlockSpec((tm, d), lambda i: (i, 0)),
                  pl.BlockSpec((None, d, d), lambda i: (layer, 0, 0)),
                  pl.BlockSpec((tm, d), lambda i: (i, 0)),
                  pl.BlockSpec((1, d), lambda i: (0, 0))],
        out_specs=pl.BlockSpec((tm, d), lambda i: (i, 0)),
        out_shape=jax.ShapeDtypeStruct((m, d), F32),
        scratch_shapes=[pltpu.VMEM((tm, d), F32)],
        compiler_params=_cparams("parallel"),
        name="out_proj",
    )(mix, w_out, x, g.reshape(1, d))


def _ffn_kernel(x_ref, g_in_ref, wg_ref, wu_ref, wd_ref, g_out_ref, o_ref, xn_ref, acc_ref):
    j = pl.program_id(1)

    @pl.when(j == 0)
    def _():
        _norm_rows_to(xn_ref, x_ref, g_in_ref, x_ref.shape[0])
        acc_ref[...] = jnp.zeros_like(acc_ref)

    xn = xn_ref[...]
    gate = jnp.dot(xn, wg_ref[...], preferred_element_type=F32)
    up = jnp.dot(xn, wu_ref[...], preferred_element_type=F32)
    act = (gate * jax.nn.sigmoid(gate) * up).astype(BF16)
    acc_ref[...] += jnp.dot(act, wd_ref[...], preferred_element_type=F32)

    @pl.when(j == pl.num_programs(1) - 1)
    def _():
        _add_norm_rows(o_ref, x_ref, acc_ref, g_out_ref, x_ref.shape[0])


def ffn(x, g_in, w_gu, w_down, layer, g_out, *, tm, tf):
    m, d = x.shape
    nf = D_FF // tf
    return pl.pallas_call(
        _ffn_kernel,
        grid=(m // tm, nf),
        in_specs=[pl.BlockSpec((tm, d), lambda i, j: (i, 0)),
                  pl.BlockSpec((1, d), lambda i, j: (0, 0)),
                  pl.BlockSpec((None, d, tf), lambda i, j: (layer, 0, j)),
                  pl.BlockSpec((None, d, tf), lambda i, j: (layer, 0, j + nf)),
                  pl.BlockSpec((None, tf, d), lambda i, j: (layer, j, 0)),
                  pl.BlockSpec((1, d), lambda i, j: (0, 0))],
        out_specs=pl.BlockSpec((tm, d), lambda i, j: (i, 0)),
        out_shape=jax.ShapeDtypeStruct((m, d), F32),
        scratch_shapes=[pltpu.VMEM((tm, d), BF16), pltpu.VMEM((tm, d), F32)],
        compiler_params=_cparams("parallel", "arbitrary"),
        name="ffn",
    )(x, g_in.reshape(1, d), w_gu, w_gu, w_down, g_out.reshape(1, d))


def _prep_w_in_t(w_in):
    splits = (CONV_DIM, CONV_DIM, CONV_DIM,
              GLA_HEADS * GLA_DK, GLA_HEADS * GLA_DK, GLA_HEADS * GLA_DV, GLA_LOWRANK, GLA_HEADS * GLA_DV,
              MLA_Q_RANK, MLA_KV_RANK, MLA_ROPE, N_BRANCH * D_MODEL)
    idx = np.cumsum(splits)[:-1].tolist()
    w_t = jnp.swapaxes(w_in, 1, 2)
    cb, cc, ch, gq, gk, gv, ga, gg, mq, mkv, mkr, gates = jnp.split(w_t, idx, axis=1)
    depth, _, d = w_t.shape
    zeros = jnp.zeros((depth, LANES - MLA_ROPE - GLA_LOWRANK + NZ - Z_END, d), w_t.dtype)
    return jnp.concatenate([gates, cb, cc, ch, gq, gk, gv, gg, mq, mkv, mkr, ga, zeros], axis=1).astype(BF16)


def _prep_wuq(wuq):
    w = wuq.reshape(MLA_Q_RANK, MLA_HEADS, MLA_NOPE + MLA_ROPE)
    nope = w[:, :, :MLA_NOPE].reshape(MLA_Q_RANK, MLA_HEADS * MLA_NOPE)
    rope = w[:, :, MLA_NOPE:].reshape(MLA_Q_RANK, MLA_HEADS * MLA_ROPE)
    return jnp.concatenate([nope, rope], axis=1).astype(BF16)


def _stacked_weights(w_in, w_branch, w_out, ffn_w_gu, ffn_w_down):
    return dict(w_in_t=_prep_w_in_t(w_in), w_branch=w_branch.astype(BF16), w_out=w_out.astype(BF16),
                w_gu=ffn_w_gu.astype(BF16), w_down=ffn_w_down.astype(BF16))


def _layer_weights(l, big, norms, conv_w, gla_wa2, gla_ba, gla_onorm, mla_q_norm, mla_kv_norm, mla_wuq,
                   mla_wuk, mla_wuv):
    return dict(
        big, layer=l, norms=norms[l], conv_w=conv_w[l],
        wa2=jnp.pad(gla_wa2[l], ((MISC_GA, LANES - MISC_GA - GLA_LOWRANK), (0, 0))).astype(BF16),
        ba=gla_ba[l], onorm=gla_onorm[l],
        q_norm=mla_q_norm[l], kv_norm=mla_kv_norm[l], wuq=_prep_wuq(mla_wuq[l]),
        wuk_t=jnp.transpose(mla_wuk[l], (1, 2, 0)).astype(BF16),
        wuv_t=jnp.transpose(mla_wuv[l], (1, 0, 2)).astype(BF16))


PROMPT_TILES = dict(in_proj_tm=1024, conv_ts=256, conv_tc=512, gla_ts=256, prep_tm=256,
                    attn_tq=128, attn_tk=512, attn_heads_per_block=2, mix_tm=1024, ffn_tm=512)
SAMPLE_TILES = dict(conv_tc=256, gla_tb=8, prep_tm=256, attn_pages_per_chunk=32)
COL_TILE = 512
OUT_PROJ_TM = 256


def _finish_layer(x2, z, ys, p, *, mix_tm, ffn_tm):
    l = p["layer"]
    mix = branch_mix(ys, z, p["w_branch"], l, tm=mix_tm, tn=COL_TILE)
    x2 = out_proj_residual(mix, p["w_out"], l, x2, p["norms"][1], tm=OUT_PROJ_TM)
    return ffn(x2, p["norms"][2], p["w_gu"], p["w_down"], l, p["norms"][3], tm=ffn_tm, tf=COL_TILE)


def _prompt_layer(x2, bsz, seq, rope_tab, p):
    t = PROMPT_TILES
    z = norm_matmul(x2, p["norms"][0], p["w_in_t"], p["layer"], tm=t["in_proj_tm"], tn=2 * NZ_TILE)
    z3 = z.reshape(bsz, seq, NZ)
    y_conv, conv_new = conv_prompt(z3, p["conv_w"], ts=t["conv_ts"], tc=t["conv_tc"])
    y_gla, gla_new = gla_prompt(z3, p["wa2"], p["ba"], p["onorm"], ts=t["gla_ts"])
    q, kv32, kvb = mla_prep(z3, rope_tab[0], rope_tab[1], p["q_norm"], p["kv_norm"], p["wuq"], p["wuk_t"],
                            tm=t["prep_tm"])
    y_mla = mla_prompt_attend(q, kvb, p["wuv_t"], tq=t["attn_tq"], tk=t["attn_tk"], hb=t["attn_heads_per_block"])
    ys = [y.reshape(bsz * seq, BRANCH_DIM) for y in (y_conv, y_gla, y_mla)]
    x2 = _finish_layer(x2, z, ys, p, mix_tm=t["mix_tm"], ffn_tm=t["ffn_tm"])
    return x2, kv32, gla_new, conv_new


def _sample_layer(x2, bsz, n_tok, rope_tab, conv_buf, gla_s0, cache_t, page_table, p):
    t = SAMPLE_TILES
    m = bsz * n_tok
    z = norm_matmul(x2, p["norms"][0], p["w_in_t"], p["layer"], tm=m, tn=2 * NZ_TILE)
    buf_rows = jnp.repeat(conv_buf.reshape(bsz, (CONV_WIDTH - 1) * CONV_DIM), n_tok, axis=0)
    y_conv, u = conv_sample(z, buf_rows, p["conv_w"], n_tok=n_tok, tc=t["conv_tc"])
    conv_new = u.reshape(bsz, n_tok, CONV_DIM)[:, n_tok - (CONV_WIDTH - 1):, :]
    y_gla, gla_new = gla_sample(z, gla_s0, p["layer"], p["wa2"], p["ba"], p["onorm"], n_tok=n_tok, tb=t["gla_tb"])
    q, kv32, kvb = mla_prep(z.reshape(1, m, NZ), rope_tab[0], rope_tab[1], p["q_norm"], p["kv_norm"],
                            p["wuq"], p["wuk_t"], tm=t["prep_tm"])
    qs = q.reshape(MLA_HEADS, bsz, n_tok, MLA_QK).transpose(1, 0, 2, 3).reshape(bsz, MLA_HEADS * n_tok, MLA_QK)
    kt_new = jnp.pad(jnp.swapaxes(kvb.reshape(bsz, n_tok, MLA_QK), 1, 2), ((0, 0), (0, 0), (0, PAGE_SIZE - n_tok)))
    o_lat = mla_sample_attend(qs, kt_new, cache_t, page_table, layer=p["layer"], n_tok=n_tok,
                              cp=t["attn_pages_per_chunk"])
    o_h = o_lat.reshape(bsz, MLA_HEADS, n_tok, MLA_KV_RANK).transpose(1, 0, 2, 3).reshape(MLA_HEADS, m, MLA_KV_RANK)
    y_mla = head_matmul(o_h, p["wuv_t"])
    x2 = _finish_layer(x2, z, [y_conv, y_gla, y_mla], p, mix_tm=m, ffn_tm=m)
    return x2, kv32.reshape(bsz, n_tok, MLA_QK), gla_new, conv_new


def kernel(x_prompt, x_sample, cache_mla, page_table, state_gla, state_conv, norms, w_in, conv_w, gla_wa2, gla_ba, gla_onorm, mla_q_norm, mla_kv_norm, mla_wuq, mla_wuk, mla_wuv, w_branch, w_out, ffn_w_gu, ffn_w_down):
    bp, sp, d = x_prompt.shape
    bs, ss, _ = x_sample.shape
    depth = norms.shape[0]
    past_len = page_table.shape[1] * PAGE_SIZE
    tab_p = _rope_tables(np.arange(sp), MLA_HEADS)
    tab_s = _rope_tables(past_len + np.arange(bs * ss) % ss, MLA_HEADS)
    xp = x_prompt.reshape(bp * sp, d)
    xs = x_sample.reshape(bs * ss, d)
    cache_t = jnp.swapaxes(cache_mla, 2, 3)
    big = _stacked_weights(w_in, w_branch, w_out, ffn_w_gu, ffn_w_down)
    outs = [[] for _ in range(6)]
    for l in range(depth):
        p = _layer_weights(l, big, norms, conv_w, gla_wa2, gla_ba, gla_onorm, mla_q_norm, mla_kv_norm, mla_wuq,
                           mla_wuk, mla_wuv)
        xp, kv_p, gla_p, conv_p = _prompt_layer(xp, bp, sp, tab_p, p)
        xs, kv_s, gla_s, conv_s = _sample_layer(xs, bs, ss, tab_s, state_conv[l], state_gla, cache_t, page_table, p)
        for acc, val in zip(outs, (kv_p, kv_s, gla_p, gla_s, conv_p, conv_s)):
            acc.append(val)
    return (xp.reshape(bp, sp, d), xs.reshape(bs, ss, d)) + tuple(jnp.stack(o) for o in outs)
```
